```python
import jax, jax.numpy as jnp
from jax import lax
import numpy as np

D_MODEL = 1024
BATCH = 8
SEQ = 4096
DEPTH = 1

D_CONV = 512
CONV_WIDTH = 31
N_HEADS = 8
HEAD_DIM = 64
N_KV = 2
HPG = N_HEADS // N_KV
D_ATTN = N_HEADS * HEAD_DIM
D_KV = N_KV * HEAD_DIM
N_BRANCH = 3
D_IN = 2 * D_CONV + D_ATTN + 2 * N_BRANCH * D_KV + N_BRANCH * N_HEADS
CMP_BLOCK = 32
CMP_STRIDE = 16
CMP_HIDDEN = 256
SEL_BLOCK = 64
SEL_TOP = 16
WINDOW = 512
Q_CHUNK = 64
D_FF = 4 * D_MODEL
EPS = 1e-6
NEG = -1e30
FORCE = 1e30

kernel_name = "hymba_conformer_nsa_hybrid"


def rmsnorm(x, g):
    xf = x.astype(jnp.float32)
    y = xf * lax.rsqrt(jnp.mean(xf * xf, axis=-1, keepdims=True) + EPS)
    return (y * g.astype(jnp.float32)).astype(x.dtype)


def layernorm(x, g, b):
    xf = x.astype(jnp.float32)
    mu = jnp.mean(xf, axis=-1, keepdims=True)
    var = jnp.mean(jnp.square(xf - mu), axis=-1, keepdims=True)
    y = (xf - mu) * lax.rsqrt(var + EPS)
    return (y * g.astype(jnp.float32) + b.astype(jnp.float32)).astype(x.dtype)


def alibi_slopes(n):
    return jnp.exp2(-8.0 * jnp.arange(1, n + 1, dtype=jnp.float32) / n)


def conformer_conv(u_val, u_gate, dw_w, dw_b, ln_g, ln_b):
    u = u_val * jax.nn.sigmoid(u_gate)
    y = lax.conv_general_dilated(
        u, dw_w.astype(u.dtype), window_strides=(1,), padding=[(CONV_WIDTH - 1, 0)],
        dimension_numbers=("NWC", "WIO", "NWC"), feature_group_count=D_CONV) + dw_b
    y = layernorm(y, ln_g, ln_b)
    return jax.nn.silu(y)


def compress(kv, pe, w1, w2):
    B, G, T, dh = kv.shape
    ch = kv.reshape(B, G, T // CMP_STRIDE, CMP_STRIDE, dh)
    blocks = jnp.concatenate([ch[:, :, :-1], ch[:, :, 1:]], axis=3) + pe
    flat = blocks.reshape(B, G, blocks.shape[2], CMP_BLOCK * dh)
    return jax.nn.gelu(flat @ w1) @ w2


def nsa_attention(q, k_cmp, v_cmp, k_sel, v_sel, k_win, v_win, gates,
                  ck_pe, ck_w1, ck_w2, cv_pe, cv_w1, cv_w2):
    B, G, _, T, dh = q.shape
    n_chunks = T // Q_CHUNK
    n_sel = T // SEL_BLOCK
    n_top = min(SEL_TOP, n_sel)
    kc = compress(k_cmp, ck_pe, ck_w1, ck_w2)
    vc = compress(v_cmp, cv_pe, cv_w1, cv_w2)
    n_cmp = kc.shape[2]
    c_idx = jnp.arange(n_cmp)
    cmp_pos = c_idx * CMP_STRIDE + (CMP_BLOCK - 1)
    j_idx = jnp.arange(n_sel)
    overlap = ((c_idx[:, None] * CMP_STRIDE < (j_idx[None, :] + 1) * SEL_BLOCK)
               & (c_idx[:, None] * CMP_STRIDE + CMP_BLOCK > j_idx[None, :] * SEL_BLOCK)
               ).astype(jnp.float32)
    kb = k_sel.reshape(B, G, n_sel, SEL_BLOCK, dh)
    vb = v_sel.reshape(B, G, n_sel, SEL_BLOCK, dh)
    kw = jnp.pad(k_win, ((0, 0), (0, 0), (WINDOW, 0), (0, 0)))
    vw = jnp.pad(v_win, ((0, 0), (0, 0), (WINDOW, 0), (0, 0)))
    slopes = alibi_slopes(N_HEADS).reshape(1, G, HPG, 1, 1)
    scale = HEAD_DIM ** -0.5
    bi = jnp.arange(B)[:, None, None, None]
    gi = jnp.arange(G)[None, :, None, None]

    def chunk(ci):
        q0 = ci * Q_CHUNK
        qc = lax.dynamic_slice_in_dim(q, q0, Q_CHUNK, axis=3) * scale
        gc = lax.dynamic_slice_in_dim(gates, q0, Q_CHUNK, axis=3)
        t = q0 + jnp.arange(Q_CHUNK)

        s = jnp.einsum("bghqd,bgcd->bghqc", qc, kc).astype(jnp.float32)
        dist = (t[:, None] - cmp_pos[None, :]).astype(jnp.float32)
        valid = dist >= 0
        s = jnp.where(valid, s - slopes * dist, NEG)
        p_cmp = jnp.where(valid, jax.nn.softmax(s, axis=-1), 0.0)
        o_cmp = jnp.einsum("bghqc,bgcd->bghqd", p_cmp, vc.astype(jnp.float32))

        imp = jnp.einsum("bghqc,cj->bgqj", p_cmp, overlap)
        jt = t // SEL_BLOCK
        future = j_idx[None, :] > jt[:, None]
        forced = ((j_idx[None, :] == 0) | (j_idx[None, :] == jt[:, None])
                  | (j_idx[None, :] == jt[:, None] - 1))
        imp = jnp.where(forced, FORCE, jnp.where(future, NEG, imp))
        _, idx = lax.top_k(imp, n_top)
        ks = kb[bi, gi, idx]
        vs = vb[bi, gi, idx]
        s_pos = idx[..., None] * SEL_BLOCK + jnp.arange(SEL_BLOCK)
        d = (t[None, None, :, None, None] - s_pos)[:, :, None].astype(jnp.float32)
        s = jnp.einsum("bghqd,bgqnkd->bghqnk", qc, ks).astype(jnp.float32)
        s = jnp.where(d >= 0, s - slopes[..., None] * d, NEG)
        s = s.reshape(B, G, HPG, Q_CHUNK, n_top * SEL_BLOCK)
        p_sel = jax.nn.softmax(s, axis=-1)
        o_sel = jnp.einsum("bghqm,bgqmd->bghqd", p_sel,
                           vs.reshape(B, G, Q_CHUNK, n_top * SEL_BLOCK, dh).astype(jnp.float32))

        kwc = lax.dynamic_slice_in_dim(kw, q0, WINDOW + Q_CHUNK, axis=2)
        vwc = lax.dynamic_slice_in_dim(vw, q0, WINDOW + Q_CHUNK, axis=2)
        w_pos = q0 - WINDOW + jnp.arange(WINDOW + Q_CHUNK)
        dw = (t[:, None] - w_pos[None, :]).astype(jnp.float32)
        valid_w = (dw >= 0) & (dw < WINDOW) & (w_pos >= 0)[None, :]
        s = jnp.einsum("bghqd,bgkd->bghqk", qc, kwc).astype(jnp.float32)
        s = jnp.where(valid_w, s - slopes * dw, NEG)
        p_win = jax.nn.softmax(s, axis=-1)
        o_win = jnp.einsum("bghqk,bgkd->bghqd", p_win, vwc.astype(jnp.float32))

        gf = gc.astype(jnp.float32)
        o = gf[..., 0:1] * o_cmp + gf[..., 1:2] * o_sel + gf[..., 2:3] * o_win
        return o.astype(q.dtype)

    outs = lax.map(chunk, jnp.arange(n_chunks))
    return outs.transpose(1, 0, 4, 2, 3, 5).reshape(B, T, N_HEADS * HEAD_DIM)


def hybrid_layer(x, norm1_g, w_in, dw_w, dw_b, cln_g, cln_b, ck_pe, ck_w1, ck_w2,
                 cv_pe, cv_w1, cv_w2, w_out, norm2_g, w_ff1, w_ff2):
    B, T, _ = x.shape
    h = rmsnorm(x, norm1_g)
    z = h @ w_in
    o1 = D_CONV
    o2 = 2 * D_CONV
    o3 = o2 + D_ATTN
    o4 = o3 + 2 * N_BRANCH * D_KV
    u_val, u_gate, q, kv, g = z[..., :o1], z[..., o1:o2], z[..., o2:o3], z[..., o3:o4], z[..., o4:]
    kv = kv.reshape(B, T, 2 * N_BRANCH, N_KV, HEAD_DIM).transpose(2, 0, 3, 1, 4)
    q = q.reshape(B, T, N_KV, HPG, HEAD_DIM).transpose(0, 2, 3, 1, 4)
    g = jax.nn.sigmoid(g).reshape(B, T, N_KV, HPG, N_BRANCH).transpose(0, 2, 3, 1, 4)
    conv_out = conformer_conv(u_val, u_gate, dw_w, dw_b, cln_g, cln_b)
    attn_out = nsa_attention(q, kv[0], kv[1], kv[2], kv[3], kv[4], kv[5], g,
                             ck_pe, ck_w1, ck_w2, cv_pe, cv_w1, cv_w2)
    x = x + jnp.concatenate([conv_out, attn_out], axis=-1) @ w_out
    h = rmsnorm(x, norm2_g)
    x = x + jnp.square(jax.nn.relu(h @ w_ff1)) @ w_ff2
    return x


def setup_inputs(seed: int = 0) -> dict:
    key = jax.random.key(seed)
    ks = jax.random.split(key, 20)
    L = DEPTH
    nrm = lambda k, shape, s: jax.random.normal(k, shape, jnp.float32) * s
    return {
        "x": jax.random.normal(ks[0], (BATCH, SEQ, D_MODEL), jnp.float32),
        "norm1_g": 1.0 + nrm(ks[1], (L, D_MODEL), 0.01),
        "w_in": nrm(ks[2], (L, D_MODEL, D_IN), D_MODEL ** -0.5),
        "dw_w": nrm(ks[3], (L, CONV_WIDTH, 1, D_CONV), CONV_WIDTH ** -0.5),
        "dw_b": nrm(ks[4], (L, D_CONV), 0.01),
        "cln_g": 1.0 + nrm(ks[5], (L, D_CONV), 0.01),
        "cln_b": nrm(ks[6], (L, D_CONV), 0.01),
        "ck_pe": nrm(ks[7], (L, CMP_BLOCK, HEAD_DIM), 0.1),
        "ck_w1": nrm(ks[8], (L, CMP_BLOCK * HEAD_DIM, CMP_HIDDEN), (CMP_BLOCK * HEAD_DIM) ** -0.5),
        "ck_w2": nrm(ks[9], (L, CMP_HIDDEN, HEAD_DIM), CMP_HIDDEN ** -0.5),
        "cv_pe": nrm(ks[10], (L, CMP_BLOCK, HEAD_DIM), 0.1),
        "cv_w1": nrm(ks[11], (L, CMP_BLOCK * HEAD_DIM, CMP_HIDDEN), (CMP_BLOCK * HEAD_DIM) ** -0.5),
        "cv_w2": nrm(ks[12], (L, CMP_HIDDEN, HEAD_DIM), CMP_HIDDEN ** -0.5),
        "w_out": nrm(ks[13], (L, D_CONV + D_ATTN, D_MODEL), (D_CONV + D_ATTN) ** -0.5),
        "norm2_g": 1.0 + nrm(ks[14], (L, D_MODEL), 0.01),
        "w_ff1": nrm(ks[15], (L, D_MODEL, D_FF), D_MODEL ** -0.5),
        "w_ff2": nrm(ks[16], (L, D_FF, D_MODEL), D_FF ** -0.5),
        "norm_f_g": 1.0 + nrm(ks[17], (D_MODEL,), 0.01),
    }


def reference(x, norm1_g, w_in, dw_w, dw_b, cln_g, cln_b, ck_pe, ck_w1, ck_w2,
              cv_pe, cv_w1, cv_w2, w_out, norm2_g, w_ff1, w_ff2, norm_f_g):
    for l in range(DEPTH):
        x = hybrid_layer(x, norm1_g[l], w_in[l], dw_w[l], dw_b[l], cln_g[l], cln_b[l],
                         ck_pe[l], ck_w1[l], ck_w2[l], cv_pe[l], cv_w1[l], cv_w2[l],
                         w_out[l], norm2_g[l], w_ff1[l], w_ff2[l])
    return rmsnorm(x, norm_f_g)
```

```python
import functools

import jax
import jax.numpy as jnp
from jax import lax
from jax.experimental import pallas as pl
from jax.experimental.pallas import tpu as pltpu

D_MODEL = 1024
D_CONV = 512
CONV_WIDTH = 31
N_HEADS = 8
HEAD_DIM = 64
N_KV = 2
HPG = N_HEADS // N_KV
D_ATTN = N_HEADS * HEAD_DIM
N_BRANCH = 3
CMP_BLOCK = 32
CMP_STRIDE = 16
CMP_HIDDEN = 256
SEL_BLOCK = 64
SEL_SHIFT = 6
SEL_TOP = 16
WINDOW = 512
Q_CHUNK = 64
D_FF = 4 * D_MODEL
EPS = 1e-6
NEG = -1e30
FORCE = 1e30
MASKED = -1e32

LANES = 128
F32 = jnp.float32
BF16 = jnp.bfloat16

TM_PROJ = 512
TM_CONV = 512
CONV_HALO = 32
CONV_ROWS = 32
TM_FFN = 512
SEL_TILE = 256
WIN_KEYS = WINDOW + Q_CHUNK
VMEM_LIMIT = 56 * 1024 * 1024

C_U = 0
C_Q = 2 * D_CONV
C_KVC = C_Q + N_HEADS * LANES
C_KV = C_KVC + 4 * HEAD_DIM
C_GATE = C_KV + 8 * LANES
C_END = C_GATE + LANES

NT_DIMS = (((1,), (1,)), ((), ()))


def _dot(a, b):
    return jnp.dot(a, b, preferred_element_type=F32)


def _dot_nt(a, b):
    return lax.dot_general(a, b, NT_DIMS, preferred_element_type=F32)


def _proj_in_kernel(x_ref, g_ref, w_ref, u_ref, q_ref, kvc_ref, kv_ref, gate_ref, *, seq):
    i = pl.program_id(0)
    x = x_ref[...]
    ms = jnp.mean(x * x, axis=-1, keepdims=True)
    h = (x * lax.rsqrt(ms + EPS) * g_ref[...]).astype(BF16)

    zu = _dot(h, w_ref[:, C_U:C_Q])
    u_ref[...] = zu[:, :D_CONV] * jax.nn.sigmoid(zu[:, D_CONV:])

    q_ref[...] = _dot(h, w_ref[:, C_Q:C_KVC]).astype(BF16)

    zc = _dot(h, w_ref[:, C_KVC:C_KV])
    for n in range(4):
        kvc_ref[n] = zc[:, n * HEAD_DIM:(n + 1) * HEAD_DIM]

    tm = x.shape[0]
    t = (i * tm) % seq + lax.broadcasted_iota(jnp.int32, (tm, LANES), 0)
    lane = lax.broadcasted_iota(jnp.int32, (tm, LANES), 1)
    feat = jnp.where(lane == LANES - 1, (t & (SEL_BLOCK - 1)).astype(F32),
                     jnp.where(lane - HEAD_DIM == (t >> SEL_SHIFT), 1.0, 0.0))
    zkv = _dot(h, w_ref[:, C_KV:C_GATE])
    for n in range(8):
        v = zkv[:, n * LANES:(n + 1) * LANES]
        if n < N_KV:
            v = v + feat
        kv_ref[n] = v.astype(BF16)

    gate_ref[...] = jax.nn.sigmoid(_dot(h, w_ref[:, C_GATE:C_END]))


def _proj_in(x2, g1, w_a, seq):
    n_tok = x2.shape[0]
    grid = (n_tok // TM_PROJ,)
    return pl.pallas_call(
        functools.partial(_proj_in_kernel, seq=seq),
        grid=grid,
        in_specs=[
            pl.BlockSpec((TM_PROJ, D_MODEL), lambda i: (i, 0)),
            pl.BlockSpec((1, D_MODEL), lambda i: (0, 0)),
            pl.BlockSpec((D_MODEL, C_END), lambda i: (0, 0), pipeline_mode=pl.Buffered(1)),
        ],
        out_specs=[
            pl.BlockSpec((TM_PROJ, D_CONV), lambda i: (i, 0)),
            pl.BlockSpec((TM_PROJ, N_HEADS * LANES), lambda i: (i, 0)),
            pl.BlockSpec((4, TM_PROJ, HEAD_DIM), lambda i: (0, i, 0)),
            pl.BlockSpec((8, TM_PROJ, LANES), lambda i: (0, i, 0)),
            pl.BlockSpec((TM_PROJ, LANES), lambda i: (i, 0)),
        ],
        out_shape=[
            jax.ShapeDtypeStruct((n_tok, D_CONV), F32),
            jax.ShapeDtypeStruct((n_tok, N_HEADS * LANES), BF16),
            jax.ShapeDtypeStruct((4, n_tok, HEAD_DIM), F32),
            jax.ShapeDtypeStruct((8, n_tok, LANES), BF16),
            jax.ShapeDtypeStruct((n_tok, LANES), F32),
        ],
        compiler_params=pltpu.CompilerParams(
            dimension_semantics=("arbitrary",), vmem_limit_bytes=VMEM_LIMIT),
        name="proj_in",
    )(x2, g1, w_a)


def _compress_kernel(a_ref, pe_ref, w1_ref, w2_ref, o_ref):
    a = a_ref[0, 0].astype(BF16)
    half = CMP_STRIDE * HEAD_DIM
    top = _dot(a, w1_ref[0, :half, :])
    bot = _dot(a, w1_ref[0, half:, :])
    n_rows = a.shape[0]
    bot = pltpu.roll(bot, n_rows - 1, 0)
    pe = _dot(pe_ref[0].astype(BF16), w1_ref[0])
    hid = jax.nn.gelu(top + bot + pe, approximate=True)
    out = _dot(hid.astype(BF16), w2_ref[0])
    row = lax.broadcasted_iota(jnp.int32, out.shape, 0)
    o_ref[0, 0] = jnp.where(row == n_rows - 1, 0.0, out).astype(BF16)


def _compress(a4, pes, w1s, w2s):
    _, batch, n_rows, width = a4.shape
    return pl.pallas_call(
        _compress_kernel,
        grid=(4, batch),
        in_specs=[
            pl.BlockSpec((1, 1, n_rows, width), lambda n, b: (n, b, 0, 0)),
            pl.BlockSpec((1, 1, CMP_BLOCK * HEAD_DIM), lambda n, b: (n // N_KV, 0, 0)),
            pl.BlockSpec((1, CMP_BLOCK * HEAD_DIM, CMP_HIDDEN), lambda n, b: (n // N_KV, 0, 0)),
            pl.BlockSpec((1, CMP_HIDDEN, LANES), lambda n, b: (n // N_KV, 0, 0)),
        ],
        out_specs=pl.BlockSpec((1, 1, n_rows, LANES), lambda n, b: (n, b, 0, 0)),
        out_shape=jax.ShapeDtypeStruct((4, batch, n_rows, LANES), BF16),
        compiler_params=pltpu.CompilerParams(
            dimension_semantics=("arbitrary", "arbitrary"), vmem_limit_bytes=VMEM_LIMIT),
        name="compress",
    )(a4, pes, w1s, w2s)


def _conv_kernel(u_ref, halo_ref, w_ref, b_ref, g_ref, beta_ref, o_ref, ext_ref):
    i = pl.program_id(1)
    halo = halo_ref[0]
    ext_ref[0:CONV_HALO, :] = jnp.where(i == 0, 0.0, halo)
    ext_ref[CONV_HALO:, :] = u_ref[0]
    tm = u_ref.shape[1]
    off = CONV_HALO - (CONV_WIDTH - 1)
    for r in range(tm // CONV_ROWS):
        r0 = r * CONV_ROWS
        acc = jnp.zeros((CONV_ROWS, D_CONV), F32) + b_ref[...]
        for k in range(CONV_WIDTH):
            acc = acc + ext_ref[r0 + off + k:r0 + off + k + CONV_ROWS, :] * w_ref[k:k + 1, :]
        mu = jnp.mean(acc, axis=-1, keepdims=True)
        d = acc - mu
        var = jnp.mean(d * d, axis=-1, keepdims=True)
        y = d * lax.rsqrt(var + EPS) * g_ref[...] + beta_ref[...]
        o_ref[0, r0:r0 + CONV_ROWS, :] = (y * jax.nn.sigmoid(y)).astype(BF16)


def _conv(u3, w, b, g, beta):
    batch, seq, _ = u3.shape
    per = TM_CONV // CONV_HALO
    return pl.pallas_call(
        _conv_kernel,
        grid=(batch, seq // TM_CONV),
        in_specs=[
            pl.BlockSpec((1, TM_CONV, D_CONV), lambda b_, i: (b_, i, 0)),
            pl.BlockSpec((1, CONV_HALO, D_CONV),
                         lambda b_, i: (b_, jnp.maximum(i * per - 1, 0), 0)),
            pl.BlockSpec((CONV_HALO, D_CONV), lambda b_, i: (0, 0)),
            pl.BlockSpec((1, D_CONV), lambda b_, i: (0, 0)),
            pl.BlockSpec((1, D_CONV), lambda b_, i: (0, 0)),
            pl.BlockSpec((1, D_CONV), lambda b_, i: (0, 0)),
        ],
        out_specs=pl.BlockSpec((1, TM_CONV, D_CONV), lambda b_, i: (b_, i, 0)),
        out_shape=jax.ShapeDtypeStruct((batch, seq, D_CONV), BF16),
        scratch_shapes=[pltpu.VMEM((TM_CONV + CONV_HALO, D_CONV), F32)],
        compiler_params=pltpu.CompilerParams(
            dimension_semantics=("arbitrary", "arbitrary"), vmem_limit_bytes=VMEM_LIMIT),
        name="conv",
    )(u3, u3, w, b, g, beta)


def _softmax_rows(s):
    m = jnp.max(s, axis=-1, keepdims=True)
    e = jnp.exp(s - m)
    return e * (1.0 / jnp.sum(e, axis=-1, keepdims=True))


def _attn_kernel(q_ref, kc_ref, vc_ref, ksel_ref, vsel_ref, kwin_ref, vwin_ref,
                 gate_ref, ovt_ref, o_ref):
    g = pl.program_id(1)
    ci = pl.program_id(2)
    q0 = ci * Q_CHUNK
    qb = q_ref[...]
    q_heads = [qb[:, h * LANES:(h + 1) * LANES] for h in range(HPG)]
    q_all = jnp.concatenate(q_heads, axis=0)
    slope0 = jnp.where(g == 0, 0.5, 0.5 ** (HPG + 1)).astype(F32)
    slopes = [slope0 * (0.5 ** h) for h in range(HPG)]

    n_cmp = kc_ref.shape[2]
    kc = kc_ref[0, 0]
    vc = vc_ref[0, 0]
    s_all = _dot_nt(q_all, kc)
    rq = lax.broadcasted_iota(jnp.int32, (Q_CHUNK, n_cmp), 0)
    cc = lax.broadcasted_iota(jnp.int32, (Q_CHUNK, n_cmp), 1)
    dist = (q0 + rq - cc * CMP_STRIDE - (CMP_BLOCK - 1)).astype(F32)
    valid = dist >= 0.0
    p_sum = jnp.zeros((Q_CHUNK, n_cmp), F32)
    o_cmp = []
    for h in range(HPG):
        s = s_all[h * Q_CHUNK:(h + 1) * Q_CHUNK]
        s = jnp.where(valid, s - slopes[h] * dist, NEG)
        p = jnp.where(valid, _softmax_rows(s), 0.0)
        p_sum = p_sum + p
        o_cmp.append(_dot(p.astype(BF16), vc))

    p_hi = p_sum.astype(BF16)
    p_lo = (p_sum - p_hi.astype(F32)).astype(BF16)
    ovt = ovt_ref[...]
    imp = _dot_nt(ovt, p_hi) + _dot_nt(ovt, p_lo)
    n_sel = imp.shape[0]
    jrow = lax.broadcasted_iota(jnp.int32, (n_sel, Q_CHUNK), 0)
    forced = (jrow == 0) | (jrow == ci) | (jrow == ci - 1)
    imp = jnp.where(forced, FORCE, jnp.where(jrow > ci, NEG, imp))
    groups = [imp[8 * v:8 * v + 8, :] for v in range(n_sel // 8)]
    sub = lax.broadcasted_iota(jnp.int32, (8, Q_CHUNK), 0)
    ranks = [jnp.zeros((8, Q_CHUNK), jnp.int32) for _ in groups]
    for jp in range(n_sel):
        row = groups[jp // 8][jp % 8:jp % 8 + 1, :]
        for v in range(n_sel // 8):
            x = groups[v]
            if 8 * v + 7 < jp:
                ranks[v] = ranks[v] + jnp.where(row > x, 1, 0)
            elif 8 * v > jp:
                ranks[v] = ranks[v] + jnp.where(row >= x, 1, 0)
            else:
                tie = jnp.where(sub + 8 * v > jp, 1, 0)
                ranks[v] = ranks[v] + jnp.where(row > x, 1, 0) + jnp.where(row == x, tie, 0)
    rank = jnp.concatenate(ranks, axis=0)
    sel_t = jnp.where((rank < SEL_TOP) & (jrow <= ci), 1.0, 0.0)
    sel_t = jnp.concatenate([jnp.zeros((LANES - n_sel, Q_CHUNK), F32), sel_t], axis=0)
    sel_t = jnp.concatenate([sel_t, jnp.zeros((LANES, LANES - Q_CHUNK), F32)], axis=1)
    sel_q = sel_t.T[:Q_CHUNK, :]
    lane = lax.broadcasted_iota(jnp.int32, (Q_CHUNK, LANES), 1)
    base = ((lane - HEAD_DIM - ci) * SEL_BLOCK).astype(F32)
    q_aug = []
    for h in range(HPG):
        feat = jnp.where(lane == LANES - 1, slopes[h],
                         jnp.where(sel_q > 0.5, slopes[h] * base, MASKED))
        q_aug.append(jnp.where(lane < HEAD_DIM, q_heads[h].astype(F32), feat).astype(BF16))
    q_aug = jnp.concatenate(q_aug, axis=0)

    def sel_tile(kt, carry, causal):
        m, l, acc = carry
        k0 = pl.multiple_of(kt * SEL_TILE, SEL_TILE)
        k = ksel_ref[0, pl.ds(k0, SEL_TILE), :]
        v = vsel_ref[0, pl.ds(k0, SEL_TILE), :]
        s = _dot_nt(q_aug, k)
        if causal:
            kpos = k0 + lax.broadcasted_iota(jnp.int32, (Q_CHUNK, SEL_TILE), 1)
            tq = q0 + lax.broadcasted_iota(jnp.int32, (Q_CHUNK, SEL_TILE), 0)
            ok = kpos <= tq
            s = jnp.concatenate(
                [jnp.where(ok, s[h * Q_CHUNK:(h + 1) * Q_CHUNK], MASKED) for h in range(HPG)],
                axis=0)
        m_new = jnp.maximum(m, jnp.max(s, axis=-1, keepdims=True))
        alpha = jnp.exp(m - m_new)
        p = jnp.exp(s - m_new)
        l = alpha * l + jnp.sum(p, axis=-1, keepdims=True)
        acc = alpha * acc + _dot(p.astype(BF16), v)
        return m_new, l, acc

    n_full = ci // (SEL_TILE // SEL_BLOCK)
    init = (jnp.full((HPG * Q_CHUNK, 1), NEG, F32),
            jnp.zeros((HPG * Q_CHUNK, 1), F32),
            jnp.zeros((HPG * Q_CHUNK, LANES), F32))
    carry = lax.fori_loop(0, n_full, lambda kt, c: sel_tile(kt, c, False), init)
    _, l_sel, acc_sel = sel_tile(n_full, carry, True)
    o_sel_all = acc_sel * (1.0 / l_sel)

    start = pl.multiple_of(jnp.maximum(ci - WINDOW // Q_CHUNK, 0) * Q_CHUNK, Q_CHUNK)
    kw = kwin_ref[0, pl.ds(start, WIN_KEYS), :]
    vw = vwin_ref[0, pl.ds(start, WIN_KEYS), :]
    s_all = _dot_nt(q_all, kw)
    rq = lax.broadcasted_iota(jnp.int32, (Q_CHUNK, WIN_KEYS), 0)
    cw = lax.broadcasted_iota(jnp.int32, (Q_CHUNK, WIN_KEYS), 1)
    dw = (q0 + rq - start - cw).astype(F32)
    valid_w = jnp.abs(dw - (WINDOW - 1) / 2) < WINDOW / 2
    o_win = []
    for h in range(HPG):
        s = s_all[h * Q_CHUNK:(h + 1) * Q_CHUNK]
        s = jnp.where(valid_w, s - slopes[h] * dw, NEG)
        o_win.append(_dot(_softmax_rows(s).astype(BF16), vw))

    gates = gate_ref[...]
    outs = []
    for h in range(HPG):
        head = g * HPG + h
        gsel = lax.broadcasted_iota(jnp.int32, (Q_CHUNK, LANES), 1)
        def gate_col(n, head=head, gsel=gsel):
            return jnp.sum(jnp.where(gsel == n * N_HEADS + head, gates, 0.0),
                           axis=-1, keepdims=True)
        o = (gate_col(0) * o_cmp[h]
             + gate_col(1) * o_sel_all[h * Q_CHUNK:(h + 1) * Q_CHUNK]
             + gate_col(2) * o_win[h])
        outs.append(o)
    pairs = [jnp.where(lane < HEAD_DIM, outs[2 * p], outs[2 * p + 1]) for p in range(HPG // 2)]
    o_ref[...] = jnp.concatenate(pairs, axis=1).astype(BF16)


def _attention(q_pad, kvc, kv, gates, ovt, batch, seq):
    n_chunks = seq // Q_CHUNK
    n_cmp = kvc.shape[2]
    n_sel = ovt.shape[0]
    kv_spec = lambda kind: pl.BlockSpec(
        (1, seq, LANES), lambda b, g, c, kind=kind: (kind * N_KV + g, b, 0))
    return pl.pallas_call(
        _attn_kernel,
        grid=(batch, N_KV, n_chunks),
        in_specs=[
            pl.BlockSpec((Q_CHUNK, HPG * LANES), lambda b, g, c: (b * n_chunks + c, g)),
            pl.BlockSpec((1, 1, n_cmp, LANES), lambda b, g, c: (g, b, 0, 0)),
            pl.BlockSpec((1, 1, n_cmp, LANES), lambda b, g, c: (N_KV + g, b, 0, 0)),
            kv_spec(0), kv_spec(1), kv_spec(2), kv_spec(3),
            pl.BlockSpec((Q_CHUNK, LANES), lambda b, g, c: (b * n_chunks + c, 0)),
            pl.BlockSpec((n_sel, n_cmp), lambda b, g, c: (0, 0)),
        ],
        out_specs=pl.BlockSpec((Q_CHUNK, HPG * HEAD_DIM), lambda b, g, c: (b * n_chunks + c, g)),
        out_shape=jax.ShapeDtypeStruct((batch * seq, D_ATTN), BF16),
        compiler_params=pltpu.CompilerParams(
            dimension_semantics=("arbitrary", "arbitrary", "arbitrary"),
            vmem_limit_bytes=VMEM_LIMIT),
        name="attn",
    )(q_pad, kvc, kvc, kv, kv, kv, kv, gates, ovt)


def _rms(x, g):
    return x * lax.rsqrt(jnp.mean(x * x, axis=-1, keepdims=True) + EPS) * g


def _out_ffn_kernel(x_ref, c_ref, a_ref, wc_ref, wa_ref, g2_ref, w1_ref, w2_ref, gf_ref, o_ref):
    x1 = x_ref[...] + _dot(c_ref[...], wc_ref[...]) + _dot(a_ref[...], wa_ref[...])
    h2 = _rms(x1, g2_ref[...]).astype(BF16)
    a = jnp.maximum(_dot(h2, w1_ref[...]), 0.0)
    y = x1 + _dot((a * a).astype(BF16), w2_ref[...])
    o_ref[...] = _rms(y, gf_ref[...])


def _out_ffn(x2, conv_out, attn_out, wc, wa, g2, w1, w2, gf):
    n_tok = x2.shape[0]
    const = lambda shape: pl.BlockSpec(shape, lambda i: (0, 0), pipeline_mode=pl.Buffered(1))
    return pl.pallas_call(
        _out_ffn_kernel,
        grid=(n_tok // TM_FFN,),
        in_specs=[
            pl.BlockSpec((TM_FFN, D_MODEL), lambda i: (i, 0)),
            pl.BlockSpec((TM_FFN, D_CONV), lambda i: (i, 0)),
            pl.BlockSpec((TM_FFN, D_ATTN), lambda i: (i, 0)),
            const((D_CONV, D_MODEL)),
            const((D_ATTN, D_MODEL)),
            const((1, D_MODEL)),
            const((D_MODEL, D_FF)),
            const((D_FF, D_MODEL)),
            const((1, D_MODEL)),
        ],
        out_specs=pl.BlockSpec((TM_FFN, D_MODEL), lambda i: (i, 0)),
        out_shape=jax.ShapeDtypeStruct((n_tok, D_MODEL), F32),
        compiler_params=pltpu.CompilerParams(
            dimension_semantics=("arbitrary",), vmem_limit_bytes=VMEM_LIMIT),
        name="out_ffn",
    )(x2, conv_out, attn_out, wc, wa, g2, w1, w2, gf)


def _pack_w_in(w_in):
    o1 = D_CONV
    o2 = 2 * D_CONV
    o3 = o2 + D_ATTN
    o4 = o3 + 2 * N_BRANCH * N_KV * HEAD_DIM
    zeros = jnp.zeros((D_MODEL, HEAD_DIM), w_in.dtype)
    cols = [w_in[:, :o2]]
    scale = HEAD_DIM ** -0.5
    for hd in range(N_HEADS):
        cols += [w_in[:, o2 + hd * HEAD_DIM:o2 + (hd + 1) * HEAD_DIM] * scale, zeros]
    kv_col = lambda n, g: w_in[:, o3 + (n * N_KV + g) * HEAD_DIM:o3 + (n * N_KV + g + 1) * HEAD_DIM]
    cols += [kv_col(0, 0), kv_col(0, 1), kv_col(1, 0), kv_col(1, 1)]
    for n in (2, 3, 4, 5):
        for g in range(N_KV):
            w = kv_col(n, g)
            cols += [w, zeros] if n % 2 == 0 else [w, w]
    gate_w = w_in[:, o4:].reshape(D_MODEL, N_HEADS, N_BRANCH).transpose(0, 2, 1)
    gate_w = gate_w.reshape(D_MODEL, N_BRANCH * N_HEADS)
    cols += [gate_w, jnp.zeros((D_MODEL, LANES - N_BRANCH * N_HEADS), w_in.dtype)]
    return jnp.concatenate(cols, axis=1).astype(BF16)


def _overlap_t(n_cmp_rows, n_sel):
    c = jnp.arange(n_cmp_rows)[None, :]
    j = jnp.arange(n_sel)[:, None]
    ov = (c * CMP_STRIDE < (j + 1) * SEL_BLOCK) & (c * CMP_STRIDE + CMP_BLOCK > j * SEL_BLOCK)
    return ov.astype(BF16)


def kernel(x, norm1_g, w_in, dw_w, dw_b, cln_g, cln_b, ck_pe, ck_w1, ck_w2, cv_pe, cv_w1, cv_w2,
           w_out, norm2_g, w_ff1, w_ff2, norm_f_g):
    batch, seq, _ = x.shape
    n_tok = batch * seq
    x2 = x.reshape(n_tok, D_MODEL)
    assert norm1_g.shape[0] == 1, "the final norm is fused into the (single) layer's last kernel"
    for l in range(1):
        w_a = _pack_w_in(w_in[l])
        u, q_pad, kvc_raw, kv, gates = _proj_in(x2, norm1_g[l][None, :], w_a, seq)

        n_rows = seq // CMP_STRIDE
        a4 = kvc_raw.reshape(4, batch, n_rows, CMP_STRIDE * HEAD_DIM)
        pes = jnp.stack([ck_pe[l], cv_pe[l]]).reshape(2, 1, CMP_BLOCK * HEAD_DIM)
        w1s = jnp.stack([ck_w1[l], cv_w1[l]]).astype(BF16)
        zero2 = jnp.zeros_like(ck_w2[l])
        w2s = jnp.stack([jnp.concatenate([ck_w2[l], zero2], axis=1),
                         jnp.concatenate([cv_w2[l], cv_w2[l]], axis=1)]).astype(BF16)
        kvc = _compress(a4, pes, w1s, w2s)

        conv_w = jnp.concatenate(
            [dw_w[l][:, 0, :], jnp.zeros((CONV_HALO - CONV_WIDTH, D_CONV), F32)], axis=0)
        conv_out = _conv(u.reshape(batch, seq, D_CONV), conv_w, dw_b[l][None, :],
                         cln_g[l][None, :], cln_b[l][None, :])

        ovt = _overlap_t(n_rows, seq // SEL_BLOCK)
        attn_out = _attention(q_pad, kvc, kv, gates, ovt, batch, seq)

        wo = w_out[l].astype(BF16)
        x2 = _out_ffn(x2, conv_out.reshape(n_tok, D_CONV), attn_out, wo[:D_CONV], wo[D_CONV:],
                      norm2_g[l][None, :], w_ff1[l].astype(BF16), w_ff2[l].astype(BF16),
                      norm_f_g[None, :])
    return x2.reshape(batch, seq, D_MODEL)
```

```python
import functools

import jax
import jax.numpy as jnp
from jax import lax
from jax.experimental import pallas as pl
from jax.experimental.pallas import tpu as pltpu

D_MODEL = 1024
D_CONV = 512
CONV_WIDTH = 31
N_HEADS = 8
HEAD_DIM = 64
N_KV = 2
HPG = N_HEADS // N_KV
D_ATTN = N_HEADS * HEAD_DIM
N_BRANCH = 3
CMP_BLOCK = 32
CMP_STRIDE = 16
CMP_HIDDEN = 256
SEL_BLOCK = 64
SEL_SHIFT = 6
SEL_TOP = 16
WINDOW = 512
D_FF = 4 * D_MODEL
EPS = 1e-6
NEG = -1e30
FORCE = 1e30
MASKED = -1e32

LANES = 128
F32 = jnp.float32
BF16 = jnp.bfloat16

TM_PROJ = 512
TM_CONV = 512
CONV_HALO = 32
CONV_ROWS = 32
TM_FFN = 512
SEL_TILE = 512
Q_TILE = 128
WIN_KEYS = WINDOW + Q_TILE
VMEM_LIMIT = 56 * 1024 * 1024

C_U = 0
C_Q = 2 * D_CONV
C_KVC = C_Q + N_HEADS * LANES
C_KV = C_KVC + 4 * HEAD_DIM
C_GATE = C_KV + 8 * LANES
C_END = C_GATE + LANES

NT_DIMS = (((1,), (1,)), ((), ()))


def _dot(a, b):
    return jnp.dot(a, b, preferred_element_type=F32)


def _dot_nt(a, b):
    return lax.dot_general(a, b, NT_DIMS, preferred_element_type=F32)


def _proj_in_kernel(x_ref, g_ref, w_ref, u_ref, q_ref, kvc_ref, kv_ref, gate_ref, *, seq):
    i = pl.program_id(0)
    x = x_ref[...]
    ms = jnp.mean(x * x, axis=-1, keepdims=True)
    h = (x * lax.rsqrt(ms + EPS) * g_ref[...]).astype(BF16)

    zu = _dot(h, w_ref[:, C_U:C_Q])
    u_ref[...] = zu[:, :D_CONV] * jax.nn.sigmoid(zu[:, D_CONV:])

    q_ref[...] = _dot(h, w_ref[:, C_Q:C_KVC]).astype(BF16)

    zc = _dot(h, w_ref[:, C_KVC:C_KV])
    for n in range(4):
        kvc_ref[n] = zc[:, n * HEAD_DIM:(n + 1) * HEAD_DIM]

    tm = x.shape[0]
    t = (i * tm) % seq + lax.broadcasted_iota(jnp.int32, (tm, LANES), 0)
    lane = lax.broadcasted_iota(jnp.int32, (tm, LANES), 1)
    feat = jnp.where(lane == LANES - 1, (t & (SEL_BLOCK - 1)).astype(F32),
                     jnp.where(lane - HEAD_DIM == (t >> SEL_SHIFT), 1.0, 0.0))
    ones_col = jnp.where(lane == HEAD_DIM, 1.0, 0.0)
    zkv = _dot(h, w_ref[:, C_KV:C_GATE])
    for n in range(8):
        v = zkv[:, n * LANES:(n + 1) * LANES]
        if n < N_KV:
            v = v + feat
        elif n < 2 * N_KV:
            v = v + ones_col
        kv_ref[n] = v.astype(BF16)

    gate_ref[...] = jax.nn.sigmoid(_dot(h, w_ref[:, C_GATE:C_END]))


def _proj_in(x2, g1, w_a, seq):
    n_tok = x2.shape[0]
    grid = (n_tok // TM_PROJ,)
    return pl.pallas_call(
        functools.partial(_proj_in_kernel, seq=seq),
        grid=grid,
        in_specs=[
            pl.BlockSpec((TM_PROJ, D_MODEL), lambda i: (i, 0)),
            pl.BlockSpec((1, D_MODEL), lambda i: (0, 0)),
            pl.BlockSpec((D_MODEL, C_END), lambda i: (0, 0), pipeline_mode=pl.Buffered(1)),
        ],
        out_specs=[
            pl.BlockSpec((TM_PROJ, D_CONV), lambda i: (i, 0)),
            pl.BlockSpec((TM_PROJ, N_HEADS * LANES), lambda i: (i, 0)),
            pl.BlockSpec((4, TM_PROJ, HEAD_DIM), lambda i: (0, i, 0)),
            pl.BlockSpec((8, TM_PROJ, LANES), lambda i: (0, i, 0)),
            pl.BlockSpec((TM_PROJ, LANES), lambda i: (i, 0)),
        ],
        out_shape=[
            jax.ShapeDtypeStruct((n_tok, D_CONV), F32),
            jax.ShapeDtypeStruct((n_tok, N_HEADS * LANES), BF16),
            jax.ShapeDtypeStruct((4, n_tok, HEAD_DIM), F32),
            jax.ShapeDtypeStruct((8, n_tok, LANES), BF16),
            jax.ShapeDtypeStruct((n_tok, LANES), F32),
        ],
        compiler_params=pltpu.CompilerParams(
            dimension_semantics=("arbitrary",), vmem_limit_bytes=VMEM_LIMIT),
        name="proj_in",
    )(x2, g1, w_a)


def _compress_kernel(a_ref, pe_ref, w1_ref, w2_ref, o_ref):
    a = a_ref[0, 0].astype(BF16)
    half = CMP_STRIDE * HEAD_DIM
    top = _dot(a, w1_ref[0, :half, :])
    bot = _dot(a, w1_ref[0, half:, :])
    n_rows = a.shape[0]
    bot = pltpu.roll(bot, n_rows - 1, 0)
    pe = _dot(pe_ref[0].astype(BF16), w1_ref[0])
    hid = jax.nn.gelu(top + bot + pe, approximate=True)
    out = _dot(hid.astype(BF16), w2_ref[0])
    row = lax.broadcasted_iota(jnp.int32, out.shape, 0)
    o_ref[0, 0] = jnp.where(row == n_rows - 1, 0.0, out).astype(BF16)


def _compress(a4, pes, w1s, w2s):
    _, batch, n_rows, width = a4.shape
    return pl.pallas_call(
        _compress_kernel,
        grid=(4, batch),
        in_specs=[
            pl.BlockSpec((1, 1, n_rows, width), lambda n, b: (n, b, 0, 0)),
            pl.BlockSpec((1, 1, CMP_BLOCK * HEAD_DIM), lambda n, b: (n // N_KV, 0, 0)),
            pl.BlockSpec((1, CMP_BLOCK * HEAD_DIM, CMP_HIDDEN), lambda n, b: (n // N_KV, 0, 0)),
            pl.BlockSpec((1, CMP_HIDDEN, LANES), lambda n, b: (n // N_KV, 0, 0)),
        ],
        out_specs=pl.BlockSpec((1, 1, n_rows, LANES), lambda n, b: (n, b, 0, 0)),
        out_shape=jax.ShapeDtypeStruct((4, batch, n_rows, LANES), BF16),
        compiler_params=pltpu.CompilerParams(
            dimension_semantics=("arbitrary", "arbitrary"), vmem_limit_bytes=VMEM_LIMIT),
        name="compress",
    )(a4, pes, w1s, w2s)


def _conv_kernel(u_ref, halo_ref, w_ref, b_ref, g_ref, beta_ref, o_ref, ext_ref):
    i = pl.program_id(1)
    halo = halo_ref[0]
    ext_ref[0:CONV_HALO, :] = jnp.where(i == 0, 0.0, halo)
    ext_ref[CONV_HALO:, :] = u_ref[0]
    tm = u_ref.shape[1]
    off = CONV_HALO - (CONV_WIDTH - 1)
    for r in range(tm // CONV_ROWS):
        r0 = r * CONV_ROWS
        acc = jnp.zeros((CONV_ROWS, D_CONV), F32) + b_ref[...]
        for k in range(CONV_WIDTH):
            acc = acc + ext_ref[r0 + off + k:r0 + off + k + CONV_ROWS, :] * w_ref[k:k + 1, :]
        mu = jnp.mean(acc, axis=-1, keepdims=True)
        d = acc - mu
        var = jnp.mean(d * d, axis=-1, keepdims=True)
        y = d * lax.rsqrt(var + EPS) * g_ref[...] + beta_ref[...]
        o_ref[0, r0:r0 + CONV_ROWS, :] = (y * jax.nn.sigmoid(y)).astype(BF16)


def _conv(u3, w, b, g, beta):
    batch, seq, _ = u3.shape
    per = TM_CONV // CONV_HALO
    return pl.pallas_call(
        _conv_kernel,
        grid=(batch, seq // TM_CONV),
        in_specs=[
            pl.BlockSpec((1, TM_CONV, D_CONV), lambda b_, i: (b_, i, 0)),
            pl.BlockSpec((1, CONV_HALO, D_CONV),
                         lambda b_, i: (b_, jnp.maximum(i * per - 1, 0), 0)),
            pl.BlockSpec((CONV_HALO, D_CONV), lambda b_, i: (0, 0)),
            pl.BlockSpec((1, D_CONV), lambda b_, i: (0, 0)),
            pl.BlockSpec((1, D_CONV), lambda b_, i: (0, 0)),
            pl.BlockSpec((1, D_CONV), lambda b_, i: (0, 0)),
        ],
        out_specs=pl.BlockSpec((1, TM_CONV, D_CONV), lambda b_, i: (b_, i, 0)),
        out_shape=jax.ShapeDtypeStruct((batch, seq, D_CONV), BF16),
        scratch_shapes=[pltpu.VMEM((TM_CONV + CONV_HALO, D_CONV), F32)],
        compiler_params=pltpu.CompilerParams(
            dimension_semantics=("arbitrary", "arbitrary"), vmem_limit_bytes=VMEM_LIMIT),
        name="conv",
    )(u3, u3, w, b, g, beta)


def _exp_rows(s):
    e = jnp.exp(s - jnp.max(s, axis=-1, keepdims=True))
    return e, 1.0 / jnp.sum(e, axis=-1, keepdims=True)


def _attn_kernel(q_ref, kc_ref, vc_ref, ksel_ref, vsel_ref, kwin_ref, vwin_ref,
                 gate_ref, ovt_ref, o_ref,
                 qa_ref, m_ref, acc_ref, s0_ref, s1_ref, p0_ref, p1_ref):
    g = pl.program_id(1)
    step = pl.program_id(2)
    q0 = step * Q_TILE
    ci0 = step * (Q_TILE // SEL_BLOCK)
    rows = HPG * Q_TILE
    qb = q_ref[...]
    q_heads = [qb[:, h * LANES:(h + 1) * LANES] for h in range(HPG)]
    q_all = jnp.concatenate(q_heads, axis=0)
    slope0 = jnp.where(g == 0, 0.5, 0.5 ** (HPG + 1)).astype(F32)
    slopes = [slope0 * (0.5 ** h) for h in range(HPG)]
    head_rows = lambda a, h: a[h * Q_TILE:(h + 1) * Q_TILE]

    start = pl.multiple_of(jnp.maximum(ci0 - WINDOW // SEL_BLOCK, 0) * SEL_BLOCK, SEL_BLOCK)
    kw = kwin_ref[0, pl.ds(start, WIN_KEYS), :]
    vw = vwin_ref[0, pl.ds(start, WIN_KEYS), :]
    s_all = _dot_nt(q_all, kw)
    rq = lax.broadcasted_iota(jnp.int32, (Q_TILE, WIN_KEYS), 0)
    cw = lax.broadcasted_iota(jnp.int32, (Q_TILE, WIN_KEYS), 1)
    dw = (q0 + rq - start - cw).astype(F32)
    valid_w = jnp.abs(dw - (WINDOW - 1) / 2) < WINDOW / 2
    o_win = []
    for h in range(HPG):
        s = jnp.where(valid_w, head_rows(s_all, h) - slopes[h] * dw, NEG)
        e, r_sum = _exp_rows(s)
        o_win.append(_dot(e.astype(BF16), vw) * r_sum)

    n_cmp = kc_ref.shape[2]
    kc = kc_ref[0, 0]
    vc = vc_ref[0, 0]
    s_all = _dot_nt(q_all, kc)
    rq = lax.broadcasted_iota(jnp.int32, (Q_TILE, n_cmp), 0)
    cc = lax.broadcasted_iota(jnp.int32, (Q_TILE, n_cmp), 1)
    dist = (q0 + rq - cc * CMP_STRIDE - (CMP_BLOCK - 1)).astype(F32)
    valid = dist >= 0.0
    any_valid = q0 + lax.broadcasted_iota(jnp.int32, (Q_TILE, 1), 0) >= CMP_BLOCK - 1
    p_sum = jnp.zeros((Q_TILE, n_cmp), F32)
    o_cmp = []
    for h in range(HPG):
        s = jnp.where(valid, head_rows(s_all, h) - slopes[h] * dist, NEG)
        e, r_sum = _exp_rows(s)
        r_sum = jnp.where(any_valid, r_sum, 0.0)
        p_sum = p_sum + e * r_sum
        o_cmp.append(_dot(e.astype(BF16), vc) * r_sum)

    p_hi = p_sum.astype(BF16)
    p_lo = (p_sum - p_hi.astype(F32)).astype(BF16)
    ovt = ovt_ref[...]
    imp = _dot_nt(ovt, p_hi) + _dot_nt(ovt, p_lo)
    n_sel = imp.shape[0]
    jrow = lax.broadcasted_iota(jnp.int32, (n_sel, Q_TILE), 0)
    qlane = lax.broadcasted_iota(jnp.int32, (n_sel, Q_TILE), 1)
    cl = ci0 + (qlane >> SEL_SHIFT)
    forced = (jrow == 0) | (jrow == cl) | (jrow == cl - 1)
    imp = jnp.where(forced, FORCE, jnp.where(jrow > cl, NEG, imp))
    groups = [imp[8 * v:8 * v + 8, :] for v in range(n_sel // 8)]
    sub = lax.broadcasted_iota(jnp.int32, (8, Q_TILE), 0)
    ranks = [jnp.zeros((8, Q_TILE), jnp.int32) for _ in groups]
    for jp in range(n_sel):
        row = groups[jp // 8][jp % 8:jp % 8 + 1, :]
        for v in range(n_sel // 8):
            x = groups[v]
            if 8 * v + 7 < jp:
                ranks[v] = ranks[v] + jnp.where(row > x, 1, 0)
            elif 8 * v > jp:
                ranks[v] = ranks[v] + jnp.where(row >= x, 1, 0)
            else:
                tie = jnp.where(sub + 8 * v > jp, 1, 0)
                ranks[v] = ranks[v] + jnp.where(row > x, 1, 0) + jnp.where(row == x, tie, 0)
    rank = jnp.concatenate(ranks, axis=0)
    sel_t = jnp.where((rank < SEL_TOP) & (jrow <= cl), 1.0, 0.0)
    sel_t = jnp.concatenate([jnp.zeros((LANES - n_sel, Q_TILE), F32), sel_t], axis=0)
    sel_q = sel_t.T
    lane = lax.broadcasted_iota(jnp.int32, (Q_TILE, LANES), 1)
    qrow = lax.broadcasted_iota(jnp.int32, (Q_TILE, LANES), 0)
    base = ((lane - HEAD_DIM - ci0 - (qrow >> SEL_SHIFT)) * SEL_BLOCK).astype(F32)
    q_aug = []
    for h in range(HPG):
        feat = jnp.where(lane == LANES - 1, slopes[h],
                         jnp.where(sel_q > 0.5, slopes[h] * base, MASKED))
        q_aug.append(jnp.where(lane < HEAD_DIM, q_heads[h].astype(F32), feat).astype(BF16))
    q_aug = jnp.concatenate(q_aug, axis=0)

    def k_tile(kt):
        return ksel_ref[0, pl.ds(pl.multiple_of(kt * SEL_TILE, SEL_TILE), SEL_TILE), :]

    def v_tile(kt):
        return vsel_ref[0, pl.ds(pl.multiple_of(kt * SEL_TILE, SEL_TILE), SEL_TILE), :]

    s_bufs = (s0_ref, s1_ref)
    p_bufs = (p0_ref, p1_ref)
    qa_ref[...] = q_aug
    m_ref[...] = jnp.full((rows, LANES), NEG, F32)
    acc_ref[...] = jnp.zeros((rows, LANES), F32)
    p1_ref[...] = jnp.zeros((rows, SEL_TILE), BF16)
    s0_ref[...] = _dot_nt(q_aug, k_tile(0))

    def online(s):
        m_old = m_ref[...]
        m_new = jnp.maximum(m_old, jnp.max(s, axis=-1, keepdims=True))
        m_ref[...] = m_new
        p = [jnp.exp(s[:, c * LANES:(c + 1) * LANES] - m_new) for c in range(SEL_TILE // LANES)]
        return jnp.exp(m_old - m_new), jnp.concatenate(p, axis=1).astype(BF16)

    def stage(i, par):
        s_bufs[1 - par][...] = _dot_nt(qa_ref[...], k_tile(i + 1))
        alpha, p = online(s_bufs[par][...])
        p_bufs[par][...] = p
        pv = _dot(p_bufs[1 - par][...], v_tile(jnp.maximum(i - 1, 0)))
        acc_ref[...] = alpha * (acc_ref[...] + pv)

    def sweep(i, carry):
        lax.cond(i % 2 == 0, lambda: stage(i, 0), lambda: stage(i, 1))
        return carry

    n_full = (ci0 + 1) // (SEL_TILE // SEL_BLOCK)
    lax.fori_loop(0, n_full, sweep, 0)
    kpos = n_full * SEL_TILE + lax.broadcasted_iota(jnp.int32, (Q_TILE, SEL_TILE), 1)
    tq = q0 + lax.broadcasted_iota(jnp.int32, (Q_TILE, SEL_TILE), 0)
    causal = kpos <= tq

    def last(par):
        s_last = s_bufs[par][...]
        s_last = jnp.concatenate(
            [jnp.where(causal, head_rows(s_last, h), MASKED) for h in range(HPG)], axis=0)
        alpha, p_last = online(s_last)
        pv = _dot(p_bufs[1 - par][...], v_tile(jnp.maximum(n_full - 1, 0)))
        acc_ref[...] = alpha * (acc_ref[...] + pv) + _dot(p_last, v_tile(n_full))

    lax.cond(n_full % 2 == 0, lambda: last(0), lambda: last(1))
    acc = acc_ref[...]
    o_sel_all = acc * (1.0 / acc[:, HEAD_DIM:HEAD_DIM + 1])

    gates = gate_ref[...]
    outs = []
    for h in range(HPG):
        head = g * HPG + h
        def gate_col(n, head=head):
            return jnp.sum(jnp.where(lane == n * N_HEADS + head, gates, 0.0),
                           axis=-1, keepdims=True)
        o_sel = head_rows(o_sel_all, h)
        if h % 2:
            o_sel = pltpu.roll(o_sel, HEAD_DIM, 1)
        outs.append(gate_col(0) * o_cmp[h] + gate_col(1) * o_sel + gate_col(2) * o_win[h])
    pairs = [jnp.where(lane < HEAD_DIM, outs[2 * p], outs[2 * p + 1]) for p in range(HPG // 2)]
    o_ref[...] = jnp.concatenate(pairs, axis=1).astype(BF16)


def _attention(q_pad, kvc, kv, gates, ovt, batch, seq):
    n_steps = seq // Q_TILE
    rows = HPG * Q_TILE
    n_cmp = kvc.shape[2]
    n_sel = ovt.shape[0]
    kv_spec = lambda kind: pl.BlockSpec(
        (1, seq, LANES), lambda b, g, c, kind=kind: (kind * N_KV + g, b, 0))
    return pl.pallas_call(
        _attn_kernel,
        grid=(batch, N_KV, n_steps),
        in_specs=[
            pl.BlockSpec((Q_TILE, HPG * LANES), lambda b, g, c: (b * n_steps + c, g)),
            pl.BlockSpec((1, 1, n_cmp, LANES), lambda b, g, c: (g, b, 0, 0)),
            pl.BlockSpec((1, 1, n_cmp, LANES), lambda b, g, c: (N_KV + g, b, 0, 0)),
            kv_spec(0), kv_spec(1), kv_spec(2), kv_spec(3),
            pl.BlockSpec((Q_TILE, LANES), lambda b, g, c: (b * n_steps + c, 0)),
            pl.BlockSpec((n_sel, n_cmp), lambda b, g, c: (0, 0)),
        ],
        out_specs=pl.BlockSpec((Q_TILE, HPG * HEAD_DIM), lambda b, g, c: (b * n_steps + c, g)),
        out_shape=jax.ShapeDtypeStruct((batch * seq, D_ATTN), BF16),
        scratch_shapes=[
            pltpu.VMEM((rows, LANES), BF16),
            pltpu.VMEM((rows, LANES), F32),
            pltpu.VMEM((rows, LANES), F32),
            pltpu.VMEM((rows, SEL_TILE), F32), pltpu.VMEM((rows, SEL_TILE), F32),
            pltpu.VMEM((rows, SEL_TILE), BF16), pltpu.VMEM((rows, SEL_TILE), BF16),
        ],
        compiler_params=pltpu.CompilerParams(
            dimension_semantics=("arbitrary", "arbitrary", "arbitrary"),
            vmem_limit_bytes=VMEM_LIMIT),
        name="attn",
    )(q_pad, kvc, kvc, kv, kv, kv, kv, gates, ovt)


def _rms(x, g):
    return x * lax.rsqrt(jnp.mean(x * x, axis=-1, keepdims=True) + EPS) * g


def _out_ffn_kernel(x_ref, c_ref, a_ref, wc_ref, wa_ref, g2_ref, w1_ref, w2_ref, gf_ref, o_ref):
    x1 = x_ref[...] + _dot(c_ref[...], wc_ref[...]) + _dot(a_ref[...], wa_ref[...])
    h2 = _rms(x1, g2_ref[...]).astype(BF16)
    a = jnp.maximum(_dot(h2, w1_ref[...]), 0.0)
    y = x1 + _dot((a * a).astype(BF16), w2_ref[...])
    o_ref[...] = _rms(y, gf_ref[...])


def _out_ffn(x2, conv_out, attn_out, wc, wa, g2, w1, w2, gf):
    n_tok = x2.shape[0]
    const = lambda shape: pl.BlockSpec(shape, lambda i: (0, 0), pipeline_mode=pl.Buffered(1))
    return pl.pallas_call(
        _out_ffn_kernel,
        grid=(n_tok // TM_FFN,),
        in_specs=[
            pl.BlockSpec((TM_FFN, D_MODEL), lambda i: (i, 0)),
            pl.BlockSpec((TM_FFN, D_CONV), lambda i: (i, 0)),
            pl.BlockSpec((TM_FFN, D_ATTN), lambda i: (i, 0)),
            const((D_CONV, D_MODEL)),
            const((D_ATTN, D_MODEL)),
            const((1, D_MODEL)),
            const((D_MODEL, D_FF)),
            const((D_FF, D_MODEL)),
            const((1, D_MODEL)),
        ],
        out_specs=pl.BlockSpec((TM_FFN, D_MODEL), lambda i: (i, 0)),
        out_shape=jax.ShapeDtypeStruct((n_tok, D_MODEL), F32),
        compiler_params=pltpu.CompilerParams(
            dimension_semantics=("arbitrary",), vmem_limit_bytes=VMEM_LIMIT),
        name="out_ffn",
    )(x2, conv_out, attn_out, wc, wa, g2, w1, w2, gf)


def _pack_w_in(w_in):
    o2 = 2 * D_CONV
    o3 = o2 + D_ATTN
    o4 = o3 + 2 * N_BRANCH * N_KV * HEAD_DIM
    zeros = jnp.zeros((D_MODEL, HEAD_DIM), w_in.dtype)
    cols = [w_in[:, :o2]]
    scale = HEAD_DIM ** -0.5
    for hd in range(N_HEADS):
        cols += [w_in[:, o2 + hd * HEAD_DIM:o2 + (hd + 1) * HEAD_DIM] * scale, zeros]
    kv_col = lambda n, g: w_in[:, o3 + (n * N_KV + g) * HEAD_DIM:o3 + (n * N_KV + g + 1) * HEAD_DIM]
    cols += [kv_col(0, 0), kv_col(0, 1), kv_col(1, 0), kv_col(1, 1)]
    for n in (2, 3, 4, 5):
        for g in range(N_KV):
            w = kv_col(n, g)
            cols += [w, w] if n == 5 else [w, zeros]
    gate_w = w_in[:, o4:].reshape(D_MODEL, N_HEADS, N_BRANCH).transpose(0, 2, 1)
    gate_w = gate_w.reshape(D_MODEL, N_BRANCH * N_HEADS)
    cols += [gate_w, jnp.zeros((D_MODEL, LANES - N_BRANCH * N_HEADS), w_in.dtype)]
    return jnp.concatenate(cols, axis=1).astype(BF16)


def _overlap_t(n_cmp_rows, n_sel):
    c = jnp.arange(n_cmp_rows)[None, :]
    j = jnp.arange(n_sel)[:, None]
    ov = (c * CMP_STRIDE < (j + 1) * SEL_BLOCK) & (c * CMP_STRIDE + CMP_BLOCK > j * SEL_BLOCK)
    return ov.astype(BF16)


def kernel(x, norm1_g, w_in, dw_w, dw_b, cln_g, cln_b, ck_pe, ck_w1, ck_w2, cv_pe, cv_w1, cv_w2,
           w_out, norm2_g, w_ff1, w_ff2, norm_f_g):
    batch, seq, _ = x.shape
    n_tok = batch * seq
    x2 = x.reshape(n_tok, D_MODEL)
    assert norm1_g.shape[0] == 1, "the final norm is fused into the (single) layer's last kernel"
    for l in range(1):
        w_a = _pack_w_in(w_in[l])
        u, q_pad, kvc_raw, kv, gates = _proj_in(x2, norm1_g[l][None, :], w_a, seq)

        n_rows = seq // CMP_STRIDE
        a4 = kvc_raw.reshape(4, batch, n_rows, CMP_STRIDE * HEAD_DIM)
        pes = jnp.stack([ck_pe[l], cv_pe[l]]).reshape(2, 1, CMP_BLOCK * HEAD_DIM)
        w1s = jnp.stack([ck_w1[l], cv_w1[l]]).astype(BF16)
        zero2 = jnp.zeros_like(ck_w2[l])
        w2s = jnp.stack([jnp.concatenate([ck_w2[l], zero2], axis=1),
                         jnp.concatenate([cv_w2[l], cv_w2[l]], axis=1)]).astype(BF16)
        kvc = _compress(a4, pes, w1s, w2s)

        conv_w = jnp.concatenate(
            [dw_w[l][:, 0, :], jnp.zeros((CONV_HALO - CONV_WIDTH, D_CONV), F32)], axis=0)
        conv_out = _conv(u.reshape(batch, seq, D_CONV), conv_w, dw_b[l][None, :],
                         cln_g[l][None, :], cln_b[l][None, :])

        ovt = _overlap_t(n_rows, seq // SEL_BLOCK)
        attn_out = _attention(q_pad, kvc, kv, gates, ovt, batch, seq)

        wo = w_out[l].astype(BF16)
        x2 = _out_ffn(x2, conv_out.reshape(n_tok, D_CONV), attn_out, wo[:D_CONV], wo[D_CONV:],
                      norm2_g[l][None, :], w_ff1[l].astype(BF16), w_ff2[l].astype(BF16),
                      norm_f_g[None, :])
    return x2.reshape(batch, seq, D_MODEL)
```

```python
import functools

import jax
import jax.numpy as jnp
from jax import lax
from jax.experimental import pallas as pl
from jax.experimental.pallas import tpu as pltpu

D_MODEL = 1024
D_CONV = 512
CONV_WIDTH = 31
N_HEADS = 8
HEAD_DIM = 64
N_KV = 2
HPG = N_HEADS // N_KV
D_ATTN = N_HEADS * HEAD_DIM
N_BRANCH = 3
CMP_BLOCK = 32
CMP_STRIDE = 16
CMP_HIDDEN = 256
SEL_BLOCK = 64
SEL_SHIFT = 6
SEL_TOP = 16
WINDOW = 512
D_FF = 4 * D_MODEL
EPS = 1e-6
NEG = -1e30
FORCE = 1e30
MASKED = -1e32

LANES = 128
SUBLANES = 8
F32 = jnp.float32
BF16 = jnp.bfloat16

TM_PROJ = 512
TM_CONV = 512
CONV_HALO = 32
CONV_ROWS = 32
TM_FFN = 512
SEL_TILE = 512
Q_TILE = 256
ROW_CHUNK = 32
RANK_WAYS = 2
F_WIN_BLK = HEAD_DIM
F_WIN_OFF = HEAD_DIM + 1
F_WIN_ONE = HEAD_DIM + 2
F_CMP_IDX = HEAD_DIM + 3
WIN_KEYS = WINDOW + Q_TILE
VMEM_LIMIT = 56 * 1024 * 1024

C_U = 0
C_Q = 2 * D_CONV
C_KVC = C_Q + N_HEADS * LANES
C_KV = C_KVC + 4 * HEAD_DIM
C_GATE = C_KV + 8 * LANES
C_END = C_GATE + LANES

NT_DIMS = (((1,), (1,)), ((), ()))


def _dot(a, b):
    return jnp.dot(a, b, preferred_element_type=F32)


def _dot_nt(a, b):
    return lax.dot_general(a, b, NT_DIMS, preferred_element_type=F32)


def _proj_in_kernel(x_ref, g_ref, w_ref, u_ref, q_ref, kvc_ref, kv_ref, gate_ref, *, seq):
    i = pl.program_id(0)
    x = x_ref[...]
    ms = jnp.mean(x * x, axis=-1, keepdims=True)
    h = (x * lax.rsqrt(ms + EPS) * g_ref[...]).astype(BF16)

    zu = _dot(h, w_ref[:, C_U:C_Q])
    u_ref[...] = zu[:, :D_CONV] * jax.nn.sigmoid(zu[:, D_CONV:])

    q_ref[...] = _dot(h, w_ref[:, C_Q:C_KVC]).astype(BF16)

    zc = _dot(h, w_ref[:, C_KVC:C_KV])
    for n in range(4):
        kvc_ref[n] = zc[:, n * HEAD_DIM:(n + 1) * HEAD_DIM]

    tm = x.shape[0]
    t = (i * tm) % seq + lax.broadcasted_iota(jnp.int32, (tm, LANES), 0)
    lane = lax.broadcasted_iota(jnp.int32, (tm, LANES), 1)
    feat = jnp.where(lane == LANES - 1, (t & (SEL_BLOCK - 1)).astype(F32),
                     jnp.where(lane - HEAD_DIM == (t >> SEL_SHIFT), 1.0, 0.0))
    ones_col = jnp.where(lane == HEAD_DIM, 1.0, 0.0)
    win_feat = jnp.where(lane == F_WIN_BLK, (t >> SEL_SHIFT).astype(F32),
                         jnp.where(lane == F_WIN_OFF, (t & (SEL_BLOCK - 1)).astype(F32),
                                   jnp.where(lane == F_WIN_ONE, 1.0, 0.0)))
    extra = (feat, ones_col, win_feat, None)
    zkv = _dot(h, w_ref[:, C_KV:C_GATE])
    for n in range(8):
        v = zkv[:, n * LANES:(n + 1) * LANES]
        if extra[n // N_KV] is not None:
            v = v + extra[n // N_KV]
        kv_ref[n] = v.astype(BF16)

    gate_ref[...] = jax.nn.sigmoid(_dot(h, w_ref[:, C_GATE:C_END]))


def _proj_in(x2, g1, w_a, seq):
    n_tok = x2.shape[0]
    grid = (n_tok // TM_PROJ,)
    return pl.pallas_call(
        functools.partial(_proj_in_kernel, seq=seq),
        grid=grid,
        in_specs=[
            pl.BlockSpec((TM_PROJ, D_MODEL), lambda i: (i, 0)),
            pl.BlockSpec((1, D_MODEL), lambda i: (0, 0)),
            pl.BlockSpec((D_MODEL, C_END), lambda i: (0, 0), pipeline_mode=pl.Buffered(1)),
        ],
        out_specs=[
            pl.BlockSpec((TM_PROJ, D_CONV), lambda i: (i, 0)),
            pl.BlockSpec((TM_PROJ, N_HEADS * LANES), lambda i: (i, 0)),
            pl.BlockSpec((4, TM_PROJ, HEAD_DIM), lambda i: (0, i, 0)),
            pl.BlockSpec((8, TM_PROJ, LANES), lambda i: (0, i, 0)),
            pl.BlockSpec((TM_PROJ, LANES), lambda i: (i, 0)),
        ],
        out_shape=[
            jax.ShapeDtypeStruct((n_tok, D_CONV), F32),
            jax.ShapeDtypeStruct((n_tok, N_HEADS * LANES), BF16),
            jax.ShapeDtypeStruct((4, n_tok, HEAD_DIM), F32),
            jax.ShapeDtypeStruct((8, n_tok, LANES), BF16),
            jax.ShapeDtypeStruct((n_tok, LANES), F32),
        ],
        compiler_params=pltpu.CompilerParams(
            dimension_semantics=("arbitrary",), vmem_limit_bytes=VMEM_LIMIT),
        name="proj_in",
    )(x2, g1, w_a)


def _compress_kernel(a_ref, pe_ref, w1_ref, w2_ref, o_ref):
    a = a_ref[0, 0].astype(BF16)
    half = CMP_STRIDE * HEAD_DIM
    top = _dot(a, w1_ref[0, :half, :])
    bot = _dot(a, w1_ref[0, half:, :])
    n_rows = a.shape[0]
    bot = pltpu.roll(bot, n_rows - 1, 0)
    pe = _dot(pe_ref[0].astype(BF16), w1_ref[0])
    hid = jax.nn.gelu(top + bot + pe, approximate=True)
    out = _dot(hid.astype(BF16), w2_ref[0])
    row = lax.broadcasted_iota(jnp.int32, out.shape, 0)
    lane = lax.broadcasted_iota(jnp.int32, out.shape, 1)
    is_key = pl.program_id(0) < N_KV
    out = out + jnp.where((lane == F_CMP_IDX) & is_key, row.astype(F32), 0.0)
    o_ref[0, 0] = jnp.where(row == n_rows - 1, 0.0, out).astype(BF16)


def _compress(a4, pes, w1s, w2s):
    _, batch, n_rows, width = a4.shape
    return pl.pallas_call(
        _compress_kernel,
        grid=(4, batch),
        in_specs=[
            pl.BlockSpec((1, 1, n_rows, width), lambda n, b: (n, b, 0, 0)),
            pl.BlockSpec((1, 1, CMP_BLOCK * HEAD_DIM), lambda n, b: (n // N_KV, 0, 0)),
            pl.BlockSpec((1, CMP_BLOCK * HEAD_DIM, CMP_HIDDEN), lambda n, b: (n // N_KV, 0, 0)),
            pl.BlockSpec((1, CMP_HIDDEN, LANES), lambda n, b: (n // N_KV, 0, 0)),
        ],
        out_specs=pl.BlockSpec((1, 1, n_rows, LANES), lambda n, b: (n, b, 0, 0)),
        out_shape=jax.ShapeDtypeStruct((4, batch, n_rows, LANES), BF16),
        compiler_params=pltpu.CompilerParams(
            dimension_semantics=("arbitrary", "arbitrary"), vmem_limit_bytes=VMEM_LIMIT),
        name="compress",
    )(a4, pes, w1s, w2s)


def _conv_kernel(u_ref, halo_ref, w_ref, b_ref, g_ref, beta_ref, o_ref, ext_ref, sh_ref):
    i = pl.program_id(1)
    halo = halo_ref[0]
    ext_ref[0:CONV_HALO, :] = jnp.where(i == 0, 0.0, halo)
    ext_ref[CONV_HALO:, :] = u_ref[0]
    tm = u_ref.shape[1]
    off = CONV_HALO - (CONV_WIDTH - 1)
    n_sh = sh_ref.shape[1]
    for b in range(1, SUBLANES):
        sh_ref[b - 1] = ext_ref[b:b + n_sh, :]
    for r in range(tm // CONV_ROWS):
        r0 = r * CONV_ROWS
        acc = jnp.zeros((CONV_ROWS, D_CONV), F32) + b_ref[...]
        for k in range(CONV_WIDTH):
            a, b = divmod(off + k, SUBLANES)
            lo = r0 + a * SUBLANES
            tap = ext_ref[lo:lo + CONV_ROWS, :] if b == 0 else sh_ref[b - 1, lo:lo + CONV_ROWS, :]
            w_k = w_ref[k * SUBLANES:(k + 1) * SUBLANES, :]
            acc = acc + tap * jnp.concatenate([w_k] * (CONV_ROWS // SUBLANES), axis=0)
        mu = jnp.mean(acc, axis=-1, keepdims=True)
        d = acc - mu
        var = jnp.mean(d * d, axis=-1, keepdims=True)
        y = d * lax.rsqrt(var + EPS) * g_ref[...] + beta_ref[...]
        o_ref[0, r0:r0 + CONV_ROWS, :] = (y * jax.nn.sigmoid(y)).astype(BF16)


def _conv(u3, w, b, g, beta):
    batch, seq, _ = u3.shape
    per = TM_CONV // CONV_HALO
    return pl.pallas_call(
        _conv_kernel,
        grid=(batch, seq // TM_CONV),
        in_specs=[
            pl.BlockSpec((1, TM_CONV, D_CONV), lambda b_, i: (b_, i, 0)),
            pl.BlockSpec((1, CONV_HALO, D_CONV),
                         lambda b_, i: (b_, jnp.maximum(i * per - 1, 0), 0)),
            pl.BlockSpec((CONV_WIDTH * SUBLANES, D_CONV), lambda b_, i: (0, 0)),
            pl.BlockSpec((1, D_CONV), lambda b_, i: (0, 0)),
            pl.BlockSpec((1, D_CONV), lambda b_, i: (0, 0)),
            pl.BlockSpec((1, D_CONV), lambda b_, i: (0, 0)),
        ],
        out_specs=pl.BlockSpec((1, TM_CONV, D_CONV), lambda b_, i: (b_, i, 0)),
        out_shape=jax.ShapeDtypeStruct((batch, seq, D_CONV), BF16),
        scratch_shapes=[
            pltpu.VMEM((TM_CONV + CONV_HALO, D_CONV), F32),
            pltpu.VMEM((SUBLANES - 1, TM_CONV + CONV_HALO - SUBLANES, D_CONV), F32),
        ],
        compiler_params=pltpu.CompilerParams(
            dimension_semantics=("arbitrary", "arbitrary"), vmem_limit_bytes=VMEM_LIMIT),
        name="conv",
    )(u3, u3, w, b, g, beta)


def _exp_rows(s):
    e = jnp.exp(s - jnp.max(s, axis=-1, keepdims=True))
    return e, 1.0 / jnp.sum(e, axis=-1, keepdims=True)


def _attn_kernel(q_ref, kc_ref, vc_ref, ksel_ref, vsel_ref, kwin_ref, vwin_ref,
                 gate_ref, ovt_ref, gexp_ref, o_ref,
                 qa_ref, m_ref, acc_ref, s0_ref, s1_ref, p0_ref, p1_ref,
                 sw_ref, ew_ref, sc_ref, ec_ref):
    g = pl.program_id(1)
    step = pl.program_id(2)
    q0 = step * Q_TILE
    ci0 = step * (Q_TILE // SEL_BLOCK)
    rows = HPG * Q_TILE
    qb = q_ref[...]
    q_heads = [qb[:, h * LANES:(h + 1) * LANES] for h in range(HPG)]
    slope0 = jnp.where(g == 0, 0.5, 0.5 ** (HPG + 1)).astype(F32)
    slopes = [slope0 * (0.5 ** h) for h in range(HPG)]
    head_rows = lambda a, h: a[h * Q_TILE:(h + 1) * Q_TILE]

    lane = lax.broadcasted_iota(jnp.int32, (Q_TILE, LANES), 1)
    n_rc = Q_TILE // ROW_CHUNK

    win_blk0 = jnp.maximum(ci0 - WINDOW // SEL_BLOCK, 0)
    start = pl.multiple_of(win_blk0 * SEL_BLOCK, SEL_BLOCK)
    q_cw = []
    for h in range(HPG):
        feat = jnp.where(
            lane == F_WIN_BLK, SEL_BLOCK * slopes[h],
            jnp.where(lane == F_WIN_OFF, slopes[h],
                      jnp.where(lane == F_WIN_ONE, -SEL_BLOCK * slopes[h] * win_blk0.astype(F32),
                                jnp.where(lane == F_CMP_IDX, CMP_STRIDE * slopes[h], 0.0))))
        q_cw.append(jnp.where(lane < HEAD_DIM, q_heads[h].astype(F32), feat).astype(BF16))
    q_cw = jnp.concatenate(q_cw, axis=0)

    n_cmp = kc_ref.shape[2]
    sc_ref[...] = _dot_nt(q_cw, kc_ref[0, 0])
    r_cmp = [[None] * n_rc for _ in range(HPG)]
    p_sum = []
    for c in range(n_rc):
        rq = c * ROW_CHUNK + lax.broadcasted_iota(jnp.int32, (ROW_CHUNK, n_cmp), 0)
        cc = lax.broadcasted_iota(jnp.int32, (ROW_CHUNK, n_cmp), 1)
        valid = q0 + rq - cc * CMP_STRIDE >= CMP_BLOCK - 1
        any_valid = (q0 + c * ROW_CHUNK
                     + lax.broadcasted_iota(jnp.int32, (ROW_CHUNK, 1), 0)) >= CMP_BLOCK - 1
        p_chunk = jnp.zeros((ROW_CHUNK, n_cmp), F32)
        for h in range(HPG):
            r0 = h * Q_TILE + c * ROW_CHUNK
            e, r_sum = _exp_rows(jnp.where(valid, sc_ref[r0:r0 + ROW_CHUNK, :], NEG))
            r_sum = jnp.where(any_valid, r_sum, 0.0)
            p_chunk = p_chunk + e * r_sum
            ec_ref[r0:r0 + ROW_CHUNK, :] = e.astype(BF16)
            r_cmp[h][c] = r_sum
        p_sum.append(p_chunk)
    p_sum = jnp.concatenate(p_sum, axis=0)
    o_cmp_all = _dot(ec_ref[...], vc_ref[0, 0])

    p_hi = p_sum.astype(BF16)
    p_lo = (p_sum - p_hi.astype(F32)).astype(BF16)
    ovt = ovt_ref[...]
    imp = _dot_nt(ovt, p_hi) + _dot_nt(ovt, p_lo)
    n_sel = imp.shape[0]
    jrow = lax.broadcasted_iota(jnp.int32, (n_sel, Q_TILE), 0)
    qlane = lax.broadcasted_iota(jnp.int32, (n_sel, Q_TILE), 1)
    cl = ci0 + (qlane >> SEL_SHIFT)
    forced = (jrow == 0) | (jrow == cl) | (jrow == cl - 1)
    imp = jnp.where(forced, FORCE, jnp.where(jrow > cl, NEG, imp))
    groups = [imp[8 * v:8 * v + 8, :] for v in range(n_sel // 8)]
    sub = lax.broadcasted_iota(jnp.int32, (8, Q_TILE), 0)
    ranks = [[jnp.zeros((8, Q_TILE), jnp.int32) for _ in range(RANK_WAYS)] for _ in groups]
    for jp in range(n_sel):
        row = groups[jp // 8][jp % 8:jp % 8 + 1, :]
        w = jp % RANK_WAYS
        for v in range(n_sel // 8):
            x = groups[v]
            if 8 * v + 7 < jp:
                ranks[v][w] = ranks[v][w] + jnp.where(row > x, 1, 0)
            elif 8 * v > jp:
                ranks[v][w] = ranks[v][w] + jnp.where(row >= x, 1, 0)
            else:
                tie = jnp.where(sub + 8 * v > jp, 1, 0)
                ranks[v][w] = (ranks[v][w] + jnp.where(row > x, 1, 0)
                               + jnp.where(row == x, tie, 0))
    rank = jnp.concatenate([sum(r[1:], r[0]) for r in ranks], axis=0)
    sel_t = jnp.where((rank < SEL_TOP) & (jrow <= cl), 1.0, 0.0)
    sel_t = jnp.concatenate([jnp.zeros((LANES - n_sel, Q_TILE), F32), sel_t], axis=0)
    sel_q = jnp.concatenate(
        [sel_t[:, c * LANES:(c + 1) * LANES].T for c in range(Q_TILE // LANES)], axis=0)
    qrow = lax.broadcasted_iota(jnp.int32, (Q_TILE, LANES), 0)
    base = ((lane - HEAD_DIM - ci0 - (qrow >> SEL_SHIFT)) * SEL_BLOCK).astype(F32)
    q_aug = []
    for h in range(HPG):
        feat = jnp.where(lane == LANES - 1, slopes[h],
                         jnp.where(sel_q > 0.5, slopes[h] * base, MASKED))
        q_aug.append(jnp.where(lane < HEAD_DIM, q_heads[h].astype(F32), feat).astype(BF16))
    q_aug = jnp.concatenate(q_aug, axis=0)

    def k_tile(kt):
        return ksel_ref[0, pl.ds(pl.multiple_of(kt * SEL_TILE, SEL_TILE), SEL_TILE), :]

    def v_tile(kt):
        return vsel_ref[0, pl.ds(pl.multiple_of(kt * SEL_TILE, SEL_TILE), SEL_TILE), :]

    s_bufs = (s0_ref, s1_ref)
    p_bufs = (p0_ref, p1_ref)
    qa_ref[...] = q_aug
    m_ref[...] = jnp.full((rows, LANES), NEG, F32)
    acc_ref[...] = jnp.zeros((rows, LANES), F32)
    p1_ref[...] = jnp.zeros((rows, SEL_TILE), BF16)
    s0_ref[...] = _dot_nt(q_aug, k_tile(0))

    sw_ref[...] = _dot_nt(q_cw, kwin_ref[0, pl.ds(start, WIN_KEYS), :])
    r_win = [[None] * n_rc for _ in range(HPG)]
    for c in range(n_rc):
        rq = c * ROW_CHUNK + lax.broadcasted_iota(jnp.int32, (ROW_CHUNK, WIN_KEYS), 0)
        dw = q0 + rq - start - lax.broadcasted_iota(jnp.int32, (ROW_CHUNK, WIN_KEYS), 1)
        valid_w = (dw | (WINDOW - 1 - dw)) >= 0
        for h in range(HPG):
            r0 = h * Q_TILE + c * ROW_CHUNK
            e, r_win[h][c] = _exp_rows(jnp.where(valid_w, sw_ref[r0:r0 + ROW_CHUNK, :], NEG))
            ew_ref[r0:r0 + ROW_CHUNK, :] = e.astype(BF16)
    o_win_all = _dot(ew_ref[...], vwin_ref[0, pl.ds(start, WIN_KEYS), :])

    n_full = (ci0 + Q_TILE // SEL_BLOCK - 1) // (SEL_TILE // SEL_BLOCK)

    def online(s_ref, p_ref, pv, causal_tile):
        for c in range(n_rc):
            if causal_tile:
                kpos = n_full * SEL_TILE + lax.broadcasted_iota(
                    jnp.int32, (ROW_CHUNK, SEL_TILE), 1)
                tq = q0 + c * ROW_CHUNK + lax.broadcasted_iota(
                    jnp.int32, (ROW_CHUNK, SEL_TILE), 0)
                causal = kpos <= tq
            for h in range(HPG):
                r0 = h * Q_TILE + c * ROW_CHUNK
                rs = slice(r0, r0 + ROW_CHUNK)
                s = s_ref[rs, :]
                if causal_tile:
                    s = jnp.where(causal, s, MASKED)
                m_old = m_ref[rs, :]
                m_new = jnp.maximum(m_old, jnp.max(s, axis=-1, keepdims=True))
                m_ref[rs, :] = m_new
                p = [jnp.exp(s[:, j * LANES:(j + 1) * LANES] - m_new)
                     for j in range(SEL_TILE // LANES)]
                p_ref[rs, :] = jnp.concatenate(p, axis=1).astype(BF16)
                acc_ref[rs, :] = jnp.exp(m_old - m_new) * (acc_ref[rs, :] + pv[rs, :])

    def stage(i, par):
        s_bufs[1 - par][...] = _dot_nt(qa_ref[...], k_tile(i + 1))
        pv = _dot(p_bufs[1 - par][...], v_tile(jnp.maximum(i - 1, 0)))
        online(s_bufs[par], p_bufs[par], pv, False)

    def sweep(i, carry):
        lax.cond(i % 2 == 0, lambda: stage(i, 0), lambda: stage(i, 1))
        return carry

    lax.fori_loop(0, n_full, sweep, 0)

    def last(par):
        pv = _dot(p_bufs[1 - par][...], v_tile(jnp.maximum(n_full - 1, 0)))
        online(s_bufs[par], p_bufs[par], pv, True)
        acc_ref[...] += _dot(p_bufs[par][...], v_tile(n_full))

    lax.cond(n_full % 2 == 0, lambda: last(0), lambda: last(1))
    acc = acc_ref[...]
    o_sel_all = acc * (1.0 / acc[:, HEAD_DIM:HEAD_DIM + 1])

    gates = gate_ref[...]
    g_hi = gates.astype(BF16)
    g_lo = (gates - g_hi.astype(F32)).astype(BF16)
    expand = gexp_ref[...]
    g_wide = _dot(g_hi, expand) + _dot(g_lo, expand)
    outs = []
    for h in range(HPG):
        def gate_col(n, h=h):
            return g_wide[:, (n * HPG + h) * LANES:(n * HPG + h + 1) * LANES]
        o_sel = head_rows(o_sel_all, h)
        if h % 2:
            o_sel = pltpu.roll(o_sel, HEAD_DIM, 1)
        o_cmp = head_rows(o_cmp_all, h) * (gate_col(0) * jnp.concatenate(r_cmp[h], axis=0))
        o_win = head_rows(o_win_all, h) * (gate_col(2) * jnp.concatenate(r_win[h], axis=0))
        outs.append(o_cmp + gate_col(1) * o_sel + o_win)
    pairs = [jnp.where(lane < HEAD_DIM, outs[2 * p], outs[2 * p + 1]) for p in range(HPG // 2)]
    o_ref[...] = jnp.concatenate(pairs, axis=1).astype(BF16)


def _gate_expand():
    col = jnp.arange(N_KV * N_BRANCH * HPG * LANES) // LANES
    g, n, h = col // (N_BRANCH * HPG), (col // HPG) % N_BRANCH, col % HPG
    row = jnp.arange(LANES)[:, None]
    return (row == (n * N_HEADS + g * HPG + h)[None, :]).astype(BF16)


def _attention(q_pad, kvc, kv, gates, ovt, batch, seq):
    n_steps = seq // Q_TILE
    rows = HPG * Q_TILE
    n_cmp = kvc.shape[2]
    n_sel = ovt.shape[0]
    kv_spec = lambda kind: pl.BlockSpec(
        (1, seq, LANES), lambda b, g, c, kind=kind: (kind * N_KV + g, b, 0))
    return pl.pallas_call(
        _attn_kernel,
        grid=(batch, N_KV, n_steps),
        in_specs=[
            pl.BlockSpec((Q_TILE, HPG * LANES), lambda b, g, c: (b * n_steps + c, g)),
            pl.BlockSpec((1, 1, n_cmp, LANES), lambda b, g, c: (g, b, 0, 0)),
            pl.BlockSpec((1, 1, n_cmp, LANES), lambda b, g, c: (N_KV + g, b, 0, 0)),
            kv_spec(0), kv_spec(1), kv_spec(2), kv_spec(3),
            pl.BlockSpec((Q_TILE, LANES), lambda b, g, c: (b * n_steps + c, 0)),
            pl.BlockSpec((n_sel, n_cmp), lambda b, g, c: (0, 0)),
            pl.BlockSpec((LANES, N_BRANCH * HPG * LANES), lambda b, g, c: (0, g)),
        ],
        out_specs=pl.BlockSpec((Q_TILE, HPG * HEAD_DIM), lambda b, g, c: (b * n_steps + c, g)),
        out_shape=jax.ShapeDtypeStruct((batch * seq, D_ATTN), BF16),
        scratch_shapes=[
            pltpu.VMEM((rows, LANES), BF16),
            pltpu.VMEM((rows, LANES), F32),
            pltpu.VMEM((rows, LANES), F32),
            pltpu.VMEM((rows, SEL_TILE), F32), pltpu.VMEM((rows, SEL_TILE), F32),
            pltpu.VMEM((rows, SEL_TILE), BF16), pltpu.VMEM((rows, SEL_TILE), BF16),
            pltpu.VMEM((rows, WIN_KEYS), F32), pltpu.VMEM((rows, WIN_KEYS), BF16),
            pltpu.VMEM((rows, n_cmp), F32), pltpu.VMEM((rows, n_cmp), BF16),
        ],
        compiler_params=pltpu.CompilerParams(
            dimension_semantics=("arbitrary", "arbitrary", "arbitrary"),
            vmem_limit_bytes=VMEM_LIMIT),
        name="attn",
    )(q_pad, kvc, kvc, kv, kv, kv, kv, gates, ovt, _gate_expand())


def _rms(x, g):
    return x * lax.rsqrt(jnp.mean(x * x, axis=-1, keepdims=True) + EPS) * g


def _out_ffn_kernel(x_ref, c_ref, a_ref, wc_ref, wa_ref, g2_ref, w1_ref, w2_ref, gf_ref, o_ref):
    x1 = x_ref[...] + _dot(c_ref[...], wc_ref[...]) + _dot(a_ref[...], wa_ref[...])
    h2 = _rms(x1, g2_ref[...]).astype(BF16)
    a = jnp.maximum(_dot(h2, w1_ref[...]), 0.0)
    y = x1 + _dot((a * a).astype(BF16), w2_ref[...])
    o_ref[...] = _rms(y, gf_ref[...])


def _out_ffn(x2, conv_out, attn_out, wc, wa, g2, w1, w2, gf):
    n_tok = x2.shape[0]
    const = lambda shape: pl.BlockSpec(shape, lambda i: (0, 0), pipeline_mode=pl.Buffered(1))
    return pl.pallas_call(
        _out_ffn_kernel,
        grid=(n_tok // TM_FFN,),
        in_specs=[
            pl.BlockSpec((TM_FFN, D_MODEL), lambda i: (i, 0)),
            pl.BlockSpec((TM_FFN, D_CONV), lambda i: (i, 0)),
            pl.BlockSpec((TM_FFN, D_ATTN), lambda i: (i, 0)),
            const((D_CONV, D_MODEL)),
            const((D_ATTN, D_MODEL)),
            const((1, D_MODEL)),
            const((D_MODEL, D_FF)),
            const((D_FF, D_MODEL)),
            const((1, D_MODEL)),
        ],
        out_specs=pl.BlockSpec((TM_FFN, D_MODEL), lambda i: (i, 0)),
        out_shape=jax.ShapeDtypeStruct((n_tok, D_MODEL), F32),
        compiler_params=pltpu.CompilerParams(
            dimension_semantics=("arbitrary",), vmem_limit_bytes=VMEM_LIMIT),
        name="out_ffn",
    )(x2, conv_out, attn_out, wc, wa, g2, w1, w2, gf)


def _pack_w_in(w_in):
    o2 = 2 * D_CONV
    o3 = o2 + D_ATTN
    o4 = o3 + 2 * N_BRANCH * N_KV * HEAD_DIM
    zeros = jnp.zeros((D_MODEL, HEAD_DIM), w_in.dtype)
    cols = [w_in[:, :o2]]
    scale = HEAD_DIM ** -0.5
    for hd in range(N_HEADS):
        cols += [w_in[:, o2 + hd * HEAD_DIM:o2 + (hd + 1) * HEAD_DIM] * scale, zeros]
    kv_col = lambda n, g: w_in[:, o3 + (n * N_KV + g) * HEAD_DIM:o3 + (n * N_KV + g + 1) * HEAD_DIM]
    cols += [kv_col(0, 0), kv_col(0, 1), kv_col(1, 0), kv_col(1, 1)]
    for n in (2, 3, 4, 5):
        for g in range(N_KV):
            w = kv_col(n, g)
            cols += [w, w] if n == 5 else [w, zeros]
    gate_w = w_in[:, o4:].reshape(D_MODEL, N_HEADS, N_BRANCH).transpose(0, 2, 1)
    gate_w = gate_w.reshape(D_MODEL, N_BRANCH * N_HEADS)
    cols += [gate_w, jnp.zeros((D_MODEL, LANES - N_BRANCH * N_HEADS), w_in.dtype)]
    return jnp.concatenate(cols, axis=1).astype(BF16)


def _overlap_t(n_cmp_rows, n_sel):
    c = jnp.arange(n_cmp_rows)[None, :]
    j = jnp.arange(n_sel)[:, None]
    ov = (c * CMP_STRIDE < (j + 1) * SEL_BLOCK) & (c * CMP_STRIDE + CMP_BLOCK > j * SEL_BLOCK)
    return ov.astype(BF16)


def kernel(x, norm1_g, w_in, dw_w, dw_b, cln_g, cln_b, ck_pe, ck_w1, ck_w2, cv_pe, cv_w1, cv_w2,
           w_out, norm2_g, w_ff1, w_ff2, norm_f_g):
    batch, seq, _ = x.shape
    n_tok = batch * seq
    x2 = x.reshape(n_tok, D_MODEL)
    assert norm1_g.shape[0] == 1, "the final norm is fused into the (single) layer's last kernel"
    for l in range(1):
        w_a = _pack_w_in(w_in[l])
        u, q_pad, kvc_raw, kv, gates = _proj_in(x2, norm1_g[l][None, :], w_a, seq)

        n_rows = seq // CMP_STRIDE
        a4 = kvc_raw.reshape(4, batch, n_rows, CMP_STRIDE * HEAD_DIM)
        pes = jnp.stack([ck_pe[l], cv_pe[l]]).reshape(2, 1, CMP_BLOCK * HEAD_DIM)
        w1s = jnp.stack([ck_w1[l], cv_w1[l]]).astype(BF16)
        zero2 = jnp.zeros_like(ck_w2[l])
        w2s = jnp.stack([jnp.concatenate([ck_w2[l], zero2], axis=1),
                         jnp.concatenate([cv_w2[l], cv_w2[l]], axis=1)]).astype(BF16)
        kvc = _compress(a4, pes, w1s, w2s)

        conv_w = jnp.repeat(dw_w[l][:, 0, :], SUBLANES, axis=0)
        conv_out = _conv(u.reshape(batch, seq, D_CONV), conv_w, dw_b[l][None, :],
                         cln_g[l][None, :], cln_b[l][None, :])

        ovt = _overlap_t(n_rows, seq // SEL_BLOCK)
        attn_out = _attention(q_pad, kvc, kv, gates, ovt, batch, seq)

        wo = w_out[l].astype(BF16)
        x2 = _out_ffn(x2, conv_out.reshape(n_tok, D_CONV), attn_out, wo[:D_CONV], wo[D_CONV:],
                      norm2_g[l][None, :], w_ff1[l].astype(BF16), w_ff2[l].astype(BF16),
                      norm_f_g[None, :])
    return x2.reshape(batch, seq, D_MODEL)
```

```python
import functools

import jax
import jax.numpy as jnp
from jax import lax
from jax.experimental import pallas as pl
from jax.experimental.pallas import tpu as pltpu

D_MODEL = 1024
D_CONV = 512
CONV_WIDTH = 31
N_HEADS = 8
HEAD_DIM = 64
N_KV = 2
HPG = N_HEADS // N_KV
D_ATTN = N_HEADS * HEAD_DIM
N_BRANCH = 3
CMP_BLOCK = 32
CMP_STRIDE = 16
CMP_HIDDEN = 256
SEL_BLOCK = 64
SEL_SHIFT = 6
SEL_TOP = 16
WINDOW = 512
D_FF = 4 * D_MODEL
EPS = 1e-6
NEG = -1e30
FORCE = 1e30
MASKED = -1e32

LANES = 128
SUBLANES = 8
F32 = jnp.float32
BF16 = jnp.bfloat16

TM_PROJ = 512
TM_CONV = 512
CONV_HALO = 32
CONV_ROWS = 32
TM_FFN = 512
SEL_TILE = 512
Q_TILE = 256
ROW_CHUNK = 32
RANK_WAYS = 2
F_WIN_BLK = HEAD_DIM
F_WIN_OFF = HEAD_DIM + 1
F_WIN_ONE = HEAD_DIM + 2
F_CMP_IDX = HEAD_DIM + 3
WIN_KEYS = WINDOW + Q_TILE
VMEM_LIMIT = 56 * 1024 * 1024

C_U = 0
C_Q = 2 * D_CONV
C_KVC = C_Q + D_ATTN
C_KV = C_KVC + 4 * HEAD_DIM
C_GATE = C_KV + 8 * HEAD_DIM
C_END = C_GATE + LANES

NT_DIMS = (((1,), (1,)), ((), ()))


def _dot(a, b):
    return jnp.dot(a, b, preferred_element_type=F32)


def _dot_nt(a, b):
    return lax.dot_general(a, b, NT_DIMS, preferred_element_type=F32)


def _proj_in_kernel(x_ref, g_ref, w_ref, u_ref, q_ref, kvc_ref, kv_ref, gate_ref, *, seq):
    i = pl.program_id(0)
    x = x_ref[...]
    ms = jnp.mean(x * x, axis=-1, keepdims=True)
    h = (x * lax.rsqrt(ms + EPS) * g_ref[...]).astype(BF16)

    zu = _dot(h, w_ref[:, C_U:C_Q])
    u_ref[...] = zu[:, :D_CONV] * jax.nn.sigmoid(zu[:, D_CONV:])

    tm = x.shape[0]
    pad64 = jnp.zeros((tm, HEAD_DIM), F32)
    zq = _dot(h, w_ref[:, C_Q:C_KVC])
    for hd in range(N_HEADS):
        q_ref[:, hd * LANES:(hd + 1) * LANES] = jnp.concatenate(
            [zq[:, hd * HEAD_DIM:(hd + 1) * HEAD_DIM], pad64], axis=1).astype(BF16)

    zc = _dot(h, w_ref[:, C_KVC:C_KV])
    for n in range(4):
        kvc_ref[n] = zc[:, n * HEAD_DIM:(n + 1) * HEAD_DIM]

    t = (i * tm) % seq + lax.broadcasted_iota(jnp.int32, (tm, LANES), 0)
    lane = lax.broadcasted_iota(jnp.int32, (tm, LANES), 1)
    feat = jnp.where(lane == LANES - 1, (t & (SEL_BLOCK - 1)).astype(F32),
                     jnp.where(lane - HEAD_DIM == (t >> SEL_SHIFT), 1.0, 0.0))
    ones_col = jnp.where(lane == HEAD_DIM, 1.0, 0.0)
    win_feat = jnp.where(lane == F_WIN_BLK, (t >> SEL_SHIFT).astype(F32),
                         jnp.where(lane == F_WIN_OFF, (t & (SEL_BLOCK - 1)).astype(F32),
                                   jnp.where(lane == F_WIN_ONE, 1.0, 0.0)))
    extra = (feat, ones_col, win_feat, None)
    zkv = _dot(h, w_ref[:, C_KV:C_GATE])
    for n in range(8):
        v = zkv[:, n * HEAD_DIM:(n + 1) * HEAD_DIM]
        v = jnp.concatenate([v, v if n // N_KV == 3 else pad64], axis=1)
        if extra[n // N_KV] is not None:
            v = v + extra[n // N_KV]
        kv_ref[n] = v.astype(BF16)

    gate_ref[...] = jax.nn.sigmoid(_dot(h, w_ref[:, C_GATE:C_END]))


def _proj_in(x2, g1, w_a, seq):
    n_tok = x2.shape[0]
    grid = (n_tok // TM_PROJ,)
    return pl.pallas_call(
        functools.partial(_proj_in_kernel, seq=seq),
        grid=grid,
        in_specs=[
            pl.BlockSpec((TM_PROJ, D_MODEL), lambda i: (i, 0)),
            pl.BlockSpec((1, D_MODEL), lambda i: (0, 0)),
            pl.BlockSpec((D_MODEL, C_END), lambda i: (0, 0), pipeline_mode=pl.Buffered(1)),
        ],
        out_specs=[
            pl.BlockSpec((TM_PROJ, D_CONV), lambda i: (i, 0)),
            pl.BlockSpec((TM_PROJ, N_HEADS * LANES), lambda i: (i, 0)),
            pl.BlockSpec((4, TM_PROJ, HEAD_DIM), lambda i: (0, i, 0)),
            pl.BlockSpec((8, TM_PROJ, LANES), lambda i: (0, i, 0)),
            pl.BlockSpec((TM_PROJ, LANES), lambda i: (i, 0)),
        ],
        out_shape=[
            jax.ShapeDtypeStruct((n_tok, D_CONV), F32),
            jax.ShapeDtypeStruct((n_tok, N_HEADS * LANES), BF16),
            jax.ShapeDtypeStruct((4, n_tok, HEAD_DIM), F32),
            jax.ShapeDtypeStruct((8, n_tok, LANES), BF16),
            jax.ShapeDtypeStruct((n_tok, LANES), F32),
        ],
        compiler_params=pltpu.CompilerParams(
            dimension_semantics=("arbitrary",), vmem_limit_bytes=VMEM_LIMIT),
        name="proj_in",
    )(x2, g1, w_a)


def _compress_kernel(a_ref, pe_ref, w1_ref, w2_ref, o_ref):
    n_rows = a_ref.shape[2] // CMP_STRIDE
    a = jnp.concatenate(
        [a_ref[0, 0, pl.ds(i, n_rows, stride=CMP_STRIDE), :] for i in range(CMP_STRIDE)],
        axis=1).astype(BF16)
    half = CMP_STRIDE * HEAD_DIM
    top = _dot(a, w1_ref[0, :half, :])
    bot = _dot(a, w1_ref[0, half:, :])
    bot = pltpu.roll(bot, n_rows - 1, 0)
    pe = _dot(pe_ref[0].astype(BF16), w1_ref[0])
    hid = jax.nn.gelu(top + bot + pe, approximate=True)
    out = _dot(hid.astype(BF16), w2_ref[0])
    row = lax.broadcasted_iota(jnp.int32, out.shape, 0)
    lane = lax.broadcasted_iota(jnp.int32, out.shape, 1)
    is_key = pl.program_id(0) < N_KV
    out = out + jnp.where((lane == F_CMP_IDX) & is_key, row.astype(F32), 0.0)
    o_ref[0, 0] = jnp.where(row == n_rows - 1, 0.0, out).astype(BF16)


def _compress(a4, pes, w1s, w2s):
    _, batch, seq, _ = a4.shape
    n_rows = seq // CMP_STRIDE
    return pl.pallas_call(
        _compress_kernel,
        grid=(4, batch),
        in_specs=[
            pl.BlockSpec((1, 1, seq, HEAD_DIM), lambda n, b: (n, b, 0, 0)),
            pl.BlockSpec((1, 1, CMP_BLOCK * HEAD_DIM), lambda n, b: (n // N_KV, 0, 0)),
            pl.BlockSpec((1, CMP_BLOCK * HEAD_DIM, CMP_HIDDEN), lambda n, b: (n // N_KV, 0, 0)),
            pl.BlockSpec((1, CMP_HIDDEN, LANES), lambda n, b: (n // N_KV, 0, 0)),
        ],
        out_specs=pl.BlockSpec((1, 1, n_rows, LANES), lambda n, b: (n, b, 0, 0)),
        out_shape=jax.ShapeDtypeStruct((4, batch, n_rows, LANES), BF16),
        compiler_params=pltpu.CompilerParams(
            dimension_semantics=("arbitrary", "arbitrary"), vmem_limit_bytes=VMEM_LIMIT),
        name="compress",
    )(a4, pes, w1s, w2s)


def _conv_kernel(u_ref, halo_ref, w_ref, b_ref, g_ref, beta_ref, o_ref, ext_ref, sh_ref):
    i = pl.program_id(1)
    halo = halo_ref[0]
    ext_ref[0:CONV_HALO, :] = jnp.where(i == 0, 0.0, halo)
    ext_ref[CONV_HALO:, :] = u_ref[0]
    tm = u_ref.shape[1]
    off = CONV_HALO - (CONV_WIDTH - 1)
    n_sh = sh_ref.shape[1]
    for b in range(1, SUBLANES):
        sh_ref[b - 1] = ext_ref[b:b + n_sh, :]
    for r in range(tm // CONV_ROWS):
        r0 = r * CONV_ROWS
        acc = jnp.zeros((CONV_ROWS, D_CONV), F32) + b_ref[...]
        for k in range(CONV_WIDTH):
            a, b = divmod(off + k, SUBLANES)
            lo = r0 + a * SUBLANES
            tap = ext_ref[lo:lo + CONV_ROWS, :] if b == 0 else sh_ref[b - 1, lo:lo + CONV_ROWS, :]
            w_k = w_ref[k * SUBLANES:(k + 1) * SUBLANES, :]
            acc = acc + tap * jnp.concatenate([w_k] * (CONV_ROWS // SUBLANES), axis=0)
        mu = jnp.mean(acc, axis=-1, keepdims=True)
        d = acc - mu
        var = jnp.mean(d * d, axis=-1, keepdims=True)
        y = d * lax.rsqrt(var + EPS) * g_ref[...] + beta_ref[...]
        o_ref[0, r0:r0 + CONV_ROWS, :] = (y * jax.nn.sigmoid(y)).astype(BF16)


def _conv(u3, w, b, g, beta):
    batch, seq, _ = u3.shape
    per = TM_CONV // CONV_HALO
    return pl.pallas_call(
        _conv_kernel,
        grid=(batch, seq // TM_CONV),
        in_specs=[
            pl.BlockSpec((1, TM_CONV, D_CONV), lambda b_, i: (b_, i, 0)),
            pl.BlockSpec((1, CONV_HALO, D_CONV),
                         lambda b_, i: (b_, jnp.maximum(i * per - 1, 0), 0)),
            pl.BlockSpec((CONV_WIDTH * SUBLANES, D_CONV), lambda b_, i: (0, 0)),
            pl.BlockSpec((1, D_CONV), lambda b_, i: (0, 0)),
            pl.BlockSpec((1, D_CONV), lambda b_, i: (0, 0)),
            pl.BlockSpec((1, D_CONV), lambda b_, i: (0, 0)),
        ],
        out_specs=pl.BlockSpec((1, TM_CONV, D_CONV), lambda b_, i: (b_, i, 0)),
        out_shape=jax.ShapeDtypeStruct((batch, seq, D_CONV), BF16),
        scratch_shapes=[
            pltpu.VMEM((TM_CONV + CONV_HALO, D_CONV), F32),
            pltpu.VMEM((SUBLANES - 1, TM_CONV + CONV_HALO - SUBLANES, D_CONV), F32),
        ],
        compiler_params=pltpu.CompilerParams(
            dimension_semantics=("arbitrary", "arbitrary"), vmem_limit_bytes=VMEM_LIMIT),
        name="conv",
    )(u3, u3, w, b, g, beta)


def _exp_rows(s):
    e = jnp.exp(s - jnp.max(s, axis=-1, keepdims=True))
    return e, 1.0 / jnp.sum(e, axis=-1, keepdims=True)


def _attn_kernel(q_ref, kc_ref, vc_ref, ksel_ref, vsel_ref, kwin_ref, vwin_ref,
                 gate_ref, ovt_ref, gexp_ref, o_ref,
                 qa_ref, m_ref, acc_ref, s0_ref, s1_ref, p0_ref, p1_ref,
                 sw_ref, ew_ref, sc_ref, ec_ref):
    g = pl.program_id(1)
    step = pl.program_id(2)
    q0 = step * Q_TILE
    ci0 = step * (Q_TILE // SEL_BLOCK)
    rows = HPG * Q_TILE
    qb = q_ref[...]
    q_heads = [qb[:, h * LANES:(h + 1) * LANES] for h in range(HPG)]
    slope0 = jnp.where(g == 0, 0.5, 0.5 ** (HPG + 1)).astype(F32)
    slopes = [slope0 * (0.5 ** h) for h in range(HPG)]
    head_rows = lambda a, h: a[h * Q_TILE:(h + 1) * Q_TILE]

    lane = lax.broadcasted_iota(jnp.int32, (Q_TILE, LANES), 1)
    n_rc = Q_TILE // ROW_CHUNK


    win_blk0 = jnp.maximum(ci0 - WINDOW // SEL_BLOCK, 0)
    start = pl.multiple_of(win_blk0 * SEL_BLOCK, SEL_BLOCK)
    q_cw = []
    for h in range(HPG):
        feat = jnp.where(
            lane == F_WIN_BLK, SEL_BLOCK * slopes[h],
            jnp.where(lane == F_WIN_OFF, slopes[h],
                      jnp.where(lane == F_WIN_ONE, -SEL_BLOCK * slopes[h] * win_blk0.astype(F32),
                                jnp.where(lane == F_CMP_IDX, CMP_STRIDE * slopes[h], 0.0))))
        q_cw.append(jnp.where(lane < HEAD_DIM, q_heads[h].astype(F32), feat).astype(BF16))
    q_cw = jnp.concatenate(q_cw, axis=0)

    n_cmp = kc_ref.shape[2]
    sc_ref[...] = _dot_nt(q_cw, kc_ref[0, 0])
    r_cmp = [[None] * n_rc for _ in range(HPG)]
    p_sum = []
    for c in range(n_rc):
        rq = c * ROW_CHUNK + lax.broadcasted_iota(jnp.int32, (ROW_CHUNK, n_cmp), 0)
        cc = lax.broadcasted_iota(jnp.int32, (ROW_CHUNK, n_cmp), 1)
        valid = q0 + rq - cc * CMP_STRIDE >= CMP_BLOCK - 1
        p_chunk = jnp.zeros((ROW_CHUNK, n_cmp), F32)
        for h in range(HPG):
            r0 = h * Q_TILE + c * ROW_CHUNK
            e, r_sum = _exp_rows(jnp.where(valid, sc_ref[r0:r0 + ROW_CHUNK, :], NEG))
            p_chunk = p_chunk + e * r_sum
            ec_ref[r0:r0 + ROW_CHUNK, :] = e.astype(BF16)
            r_cmp[h][c] = r_sum
        p_sum.append(p_chunk)
    p_sum = jnp.concatenate(p_sum, axis=0)
    o_cmp_all = _dot(ec_ref[...], vc_ref[0, 0])

    p_hi = p_sum.astype(BF16)
    p_lo = (p_sum - p_hi.astype(F32)).astype(BF16)
    ovt = ovt_ref[...]
    imp = _dot_nt(ovt, p_hi) + _dot_nt(ovt, p_lo)
    n_sel = imp.shape[0]
    jrow = lax.broadcasted_iota(jnp.int32, (n_sel, Q_TILE), 0)
    qlane = lax.broadcasted_iota(jnp.int32, (n_sel, Q_TILE), 1)
    cl = ci0 + (qlane >> SEL_SHIFT)
    forced = (jrow == 0) | (jrow == cl) | (jrow == cl - 1)
    imp = jnp.where(forced, FORCE, jnp.where(jrow > cl, NEG, imp))
    sw_ref[...] = _dot_nt(q_cw, kwin_ref[0, pl.ds(start, WIN_KEYS), :])
    groups = [imp[8 * v:8 * v + 8, :] for v in range(n_sel // 8)]
    sub = lax.broadcasted_iota(jnp.int32, (8, Q_TILE), 0)
    ranks = [[jnp.zeros((8, Q_TILE), jnp.int32) for _ in range(RANK_WAYS)] for _ in groups]
    for jp in range(n_sel):
        row = groups[jp // 8][jp % 8:jp % 8 + 1, :]
        w = jp % RANK_WAYS
        for v in range(n_sel // 8):
            x = groups[v]
            if 8 * v + 7 < jp:
                ranks[v][w] = ranks[v][w] + jnp.where(row > x, 1, 0)
            elif 8 * v > jp:
                ranks[v][w] = ranks[v][w] + jnp.where(row >= x, 1, 0)
            else:
                tie = jnp.where(sub + 8 * v > jp, 1, 0)
                ranks[v][w] = (ranks[v][w] + jnp.where(row > x, 1, 0)
                               + jnp.where(row == x, tie, 0))
    rank = jnp.concatenate([sum(r[1:], r[0]) for r in ranks], axis=0)
    sel_t = jnp.where((rank < SEL_TOP) & (jrow <= cl), 1.0, 0.0)
    sel_t = jnp.concatenate([jnp.zeros((LANES - n_sel, Q_TILE), F32), sel_t], axis=0)
    sel_q = jnp.concatenate(
        [sel_t[:, c * LANES:(c + 1) * LANES].T for c in range(Q_TILE // LANES)], axis=0)
    qrow = lax.broadcasted_iota(jnp.int32, (Q_TILE, LANES), 0)
    base = ((lane - HEAD_DIM - ci0 - (qrow >> SEL_SHIFT)) * SEL_BLOCK).astype(F32)
    q_aug = []
    for h in range(HPG):
        feat = jnp.where(lane == LANES - 1, slopes[h],
                         jnp.where(sel_q > 0.5, slopes[h] * base, MASKED))
        q_aug.append(jnp.where(lane < HEAD_DIM, q_heads[h].astype(F32), feat).astype(BF16))
    q_aug = jnp.concatenate(q_aug, axis=0)

    def k_tile(kt):
        return ksel_ref[0, pl.ds(pl.multiple_of(kt * SEL_TILE, SEL_TILE), SEL_TILE), :]

    def v_tile(kt):
        return vsel_ref[0, pl.ds(pl.multiple_of(kt * SEL_TILE, SEL_TILE), SEL_TILE), :]

    s_bufs = (s0_ref, s1_ref)
    p_bufs = (p0_ref, p1_ref)
    qa_ref[...] = q_aug
    m_ref[...] = jnp.full((rows, LANES), NEG, F32)
    acc_ref[...] = jnp.zeros((rows, LANES), F32)
    p1_ref[...] = jnp.zeros((rows, SEL_TILE), BF16)
    s0_ref[...] = _dot_nt(q_aug, k_tile(0))

    r_win = [[None] * n_rc for _ in range(HPG)]
    o_win_all = []
    for pair in range(HPG // 2):
        for c in range(n_rc):
            rq = c * ROW_CHUNK + lax.broadcasted_iota(jnp.int32, (ROW_CHUNK, WIN_KEYS), 0)
            dw = q0 + rq - start - lax.broadcasted_iota(jnp.int32, (ROW_CHUNK, WIN_KEYS), 1)
            valid_w = (dw | (WINDOW - 1 - dw)) >= 0
            for h in (2 * pair, 2 * pair + 1):
                r0 = h * Q_TILE + c * ROW_CHUNK
                e, r_win[h][c] = _exp_rows(
                    jnp.where(valid_w, sw_ref[r0:r0 + ROW_CHUNK, :], NEG))
                ew_ref[r0:r0 + ROW_CHUNK, :] = e.astype(BF16)
        pair_rows = slice(2 * pair * Q_TILE, (2 * pair + 2) * Q_TILE)
        o_win_all.append(_dot(ew_ref[pair_rows, :], vwin_ref[0, pl.ds(start, WIN_KEYS), :]))
    o_win_all = jnp.concatenate(o_win_all, axis=0)

    n_full = (ci0 + Q_TILE // SEL_BLOCK - 1) // (SEL_TILE // SEL_BLOCK)

    def online(s_ref, p_ref, pv, causal_tile):
        for c in range(n_rc):
            if causal_tile:
                kpos = n_full * SEL_TILE + lax.broadcasted_iota(
                    jnp.int32, (ROW_CHUNK, SEL_TILE), 1)
                tq = q0 + c * ROW_CHUNK + lax.broadcasted_iota(
                    jnp.int32, (ROW_CHUNK, SEL_TILE), 0)
                causal = kpos <= tq
            for h in range(HPG):
                r0 = h * Q_TILE + c * ROW_CHUNK
                rs = slice(r0, r0 + ROW_CHUNK)
                s = s_ref[rs, :]
                if causal_tile:
                    s = jnp.where(causal, s, MASKED)
                m_old = m_ref[rs, :]
                m_new = jnp.maximum(m_old, jnp.max(s, axis=-1, keepdims=True))
                m_ref[rs, :] = m_new
                p = [jnp.exp(s[:, j * LANES:(j + 1) * LANES] - m_new)
                     for j in range(SEL_TILE // LANES)]
                p_ref[rs, :] = jnp.concatenate(p, axis=1).astype(BF16)
                acc_ref[rs, :] = jnp.exp(m_old - m_new) * (acc_ref[rs, :] + pv[rs, :])

    def stage(i, par):
        s_bufs[1 - par][...] = _dot_nt(qa_ref[...], k_tile(i + 1))
        pv = _dot(p_bufs[1 - par][...], v_tile(jnp.maximum(i - 1, 0)))
        online(s_bufs[par], p_bufs[par], pv, False)

    def sweep(i, carry):
        lax.cond(i % 2 == 0, lambda: stage(i, 0), lambda: stage(i, 1))
        return carry

    lax.fori_loop(0, n_full, sweep, 0)

    def last(par):
        pv = _dot(p_bufs[1 - par][...], v_tile(jnp.maximum(n_full - 1, 0)))
        online(s_bufs[par], p_bufs[par], pv, True)
        acc_ref[...] += _dot(p_bufs[par][...], v_tile(n_full))

    lax.cond(n_full % 2 == 0, lambda: last(0), lambda: last(1))
    acc = acc_ref[...]
    o_sel_all = acc * (1.0 / acc[:, HEAD_DIM:HEAD_DIM + 1])

    gates = gate_ref[...]
    g_hi = gates.astype(BF16)
    g_lo = (gates - g_hi.astype(F32)).astype(BF16)
    expand = gexp_ref[...]
    g_wide = _dot(g_hi, expand) + _dot(g_lo, expand)
    even_lanes = lane < HEAD_DIM
    sees_cmp = q0 + lax.broadcasted_iota(jnp.int32, (Q_TILE, LANES), 0) >= CMP_BLOCK - 1
    pairs = []
    for pair in range(HPG // 2):
        h0, h1 = 2 * pair, 2 * pair + 1
        def gate(n, pair=pair):
            blk = n * (HPG // 2) + pair
            return g_wide[:, blk * LANES:(blk + 1) * LANES]
        def both(x0, x1):
            return jnp.where(even_lanes, x0, x1)
        o_cmp = both(head_rows(o_cmp_all, h0) * jnp.concatenate(r_cmp[h0], axis=0),
                     head_rows(o_cmp_all, h1) * jnp.concatenate(r_cmp[h1], axis=0))
        o_cmp = jnp.where(sees_cmp, o_cmp, 0.0)
        o_win = both(head_rows(o_win_all, h0) * jnp.concatenate(r_win[h0], axis=0),
                     head_rows(o_win_all, h1) * jnp.concatenate(r_win[h1], axis=0))
        o_sel = both(head_rows(o_sel_all, h0), pltpu.roll(head_rows(o_sel_all, h1), HEAD_DIM, 1))
        pairs.append(gate(0) * o_cmp + gate(1) * o_sel + gate(2) * o_win)
    o_ref[...] = jnp.concatenate(pairs, axis=1).astype(BF16)


def _gate_expand():
    col = jnp.arange(N_KV * N_BRANCH * (HPG // 2) * LANES)
    blk, h_in_pair = col // LANES, (col % LANES) // HEAD_DIM
    g, n, pair = blk // (N_BRANCH * HPG // 2), (blk // (HPG // 2)) % N_BRANCH, blk % (HPG // 2)
    row = jnp.arange(LANES)[:, None]
    return (row == (n * N_HEADS + g * HPG + 2 * pair + h_in_pair)[None, :]).astype(BF16)


def _attention(q_pad, kvc, kv, gates, ovt, batch, seq):
    n_steps = seq // Q_TILE
    rows = HPG * Q_TILE
    n_cmp = kvc.shape[2]
    n_sel = ovt.shape[0]
    kv_spec = lambda kind: pl.BlockSpec(
        (1, seq, LANES), lambda b, g, c, kind=kind: (kind * N_KV + g, b, 0))
    return pl.pallas_call(
        _attn_kernel,
        grid=(batch, N_KV, n_steps),
        in_specs=[
            pl.BlockSpec((Q_TILE, HPG * LANES), lambda b, g, c: (b * n_steps + c, g)),
            pl.BlockSpec((1, 1, n_cmp, LANES), lambda b, g, c: (g, b, 0, 0)),
            pl.BlockSpec((1, 1, n_cmp, LANES), lambda b, g, c: (N_KV + g, b, 0, 0)),
            kv_spec(0), kv_spec(1), kv_spec(2), kv_spec(3),
            pl.BlockSpec((Q_TILE, LANES), lambda b, g, c: (b * n_steps + c, 0)),
            pl.BlockSpec((n_sel, n_cmp), lambda b, g, c: (0, 0)),
            pl.BlockSpec((LANES, N_BRANCH * (HPG // 2) * LANES), lambda b, g, c: (0, g)),
        ],
        out_specs=pl.BlockSpec((Q_TILE, HPG * HEAD_DIM), lambda b, g, c: (b * n_steps + c, g)),
        out_shape=jax.ShapeDtypeStruct((batch * seq, D_ATTN), BF16),
        scratch_shapes=[
            pltpu.VMEM((rows, LANES), BF16),
            pltpu.VMEM((rows, LANES), F32),
            pltpu.VMEM((rows, LANES), F32),
            pltpu.VMEM((rows, SEL_TILE), F32), pltpu.VMEM((rows, SEL_TILE), F32),
            pltpu.VMEM((rows, SEL_TILE), BF16), pltpu.VMEM((rows, SEL_TILE), BF16),
            pltpu.VMEM((rows, WIN_KEYS), F32), pltpu.VMEM((rows, WIN_KEYS), BF16),
            pltpu.VMEM((rows, n_cmp), F32), pltpu.VMEM((rows, n_cmp), BF16),
        ],
        compiler_params=pltpu.CompilerParams(
            dimension_semantics=("arbitrary", "arbitrary", "arbitrary"),
            vmem_limit_bytes=VMEM_LIMIT),
        name="attn",
    )(q_pad, kvc, kvc, kv, kv, kv, kv, gates, ovt, _gate_expand())


def _rms(x, g):
    return x * lax.rsqrt(jnp.mean(x * x, axis=-1, keepdims=True) + EPS) * g


def _out_ffn_kernel(x_ref, c_ref, a_ref, wc_ref, wa_ref, g2_ref, w1_ref, w2_ref, gf_ref, o_ref):
    x1 = x_ref[...] + _dot(c_ref[...], wc_ref[...]) + _dot(a_ref[...], wa_ref[...])
    h2 = _rms(x1, g2_ref[...]).astype(BF16)
    a = jnp.maximum(_dot(h2, w1_ref[...]), 0.0)
    y = x1 + _dot((a * a).astype(BF16), w2_ref[...])
    o_ref[...] = _rms(y, gf_ref[...])


def _out_ffn(x2, conv_out, attn_out, wc, wa, g2, w1, w2, gf):
    n_tok = x2.shape[0]
    const = lambda shape: pl.BlockSpec(shape, lambda i: (0, 0), pipeline_mode=pl.Buffered(1))
    return pl.pallas_call(
        _out_ffn_kernel,
        grid=(n_tok // TM_FFN,),
        in_specs=[
            pl.BlockSpec((TM_FFN, D_MODEL), lambda i: (i, 0)),
            pl.BlockSpec((TM_FFN, D_CONV), lambda i: (i, 0)),
            pl.BlockSpec((TM_FFN, D_ATTN), lambda i: (i, 0)),
            const((D_CONV, D_MODEL)),
            const((D_ATTN, D_MODEL)),
            const((1, D_MODEL)),
            const((D_MODEL, D_FF)),
            const((D_FF, D_MODEL)),
            const((1, D_MODEL)),
        ],
        out_specs=pl.BlockSpec((TM_FFN, D_MODEL), lambda i: (i, 0)),
        out_shape=jax.ShapeDtypeStruct((n_tok, D_MODEL), F32),
        compiler_params=pltpu.CompilerParams(
            dimension_semantics=("arbitrary",), vmem_limit_bytes=VMEM_LIMIT),
        name="out_ffn",
    )(x2, conv_out, attn_out, wc, wa, g2, w1, w2, gf)


def _pack_w_in(w_in):
    o2 = 2 * D_CONV
    o3 = o2 + D_ATTN
    o4 = o3 + 2 * N_BRANCH * N_KV * HEAD_DIM
    gate_w = w_in[:, o4:].reshape(D_MODEL, N_HEADS, N_BRANCH).transpose(0, 2, 1)
    gate_w = gate_w.reshape(D_MODEL, N_BRANCH * N_HEADS)
    cols = [w_in[:, :o2], w_in[:, o2:o3] * HEAD_DIM ** -0.5, w_in[:, o3:o4], gate_w,
            jnp.zeros((D_MODEL, LANES - N_BRANCH * N_HEADS), w_in.dtype)]
    return jnp.concatenate(cols, axis=1).astype(BF16)


def _overlap_t(n_cmp_rows, n_sel):
    c = jnp.arange(n_cmp_rows)[None, :]
    j = jnp.arange(n_sel)[:, None]
    ov = (c * CMP_STRIDE < (j + 1) * SEL_BLOCK) & (c * CMP_STRIDE + CMP_BLOCK > j * SEL_BLOCK)
    return ov.astype(BF16)


def kernel(x, norm1_g, w_in, dw_w, dw_b, cln_g, cln_b, ck_pe, ck_w1, ck_w2, cv_pe, cv_w1, cv_w2,
           w_out, norm2_g, w_ff1, w_ff2, norm_f_g):
    batch, seq, _ = x.shape
    n_tok = batch * seq
    x2 = x.reshape(n_tok, D_MODEL)
    assert norm1_g.shape[0] == 1, "the final norm is fused into the (single) layer's last kernel"
    for l in range(1):
        w_a = _pack_w_in(w_in[l])
        u, q_pad, kvc_raw, kv, gates = _proj_in(x2, norm1_g[l][None, :], w_a, seq)

        n_rows = seq // CMP_STRIDE
        a4 = kvc_raw.reshape(4, batch, seq, HEAD_DIM)
        pes = jnp.stack([ck_pe[l], cv_pe[l]]).reshape(2, 1, CMP_BLOCK * HEAD_DIM)
        w1s = jnp.stack([ck_w1[l], cv_w1[l]]).astype(BF16)
        zero2 = jnp.zeros_like(ck_w2[l])
        w2s = jnp.stack([jnp.concatenate([ck_w2[l], zero2], axis=1),
                         jnp.concatenate([cv_w2[l], cv_w2[l]], axis=1)]).astype(BF16)
        kvc = _compress(a4, pes, w1s, w2s)

        conv_w = jnp.repeat(dw_w[l][:, 0, :], SUBLANES, axis=0)
        conv_out = _conv(u.reshape(batch, seq, D_CONV), conv_w, dw_b[l][None, :],
                         cln_g[l][None, :], cln_b[l][None, :])

        ovt = _overlap_t(n_rows, seq // SEL_BLOCK)
        attn_out = _attention(q_pad, kvc, kv, gates, ovt, batch, seq)

        wo = w_out[l].astype(BF16)
        x2 = _out_ffn(x2, conv_out.reshape(n_tok, D_CONV), attn_out, wo[:D_CONV], wo[D_CONV:],
                      norm2_g[l][None, :], w_ff1[l].astype(BF16), w_ff2[l].astype(BF16),
                      norm_f_g[None, :])
    return x2.reshape(batch, seq, D_MODEL)
```

```python
import functools

import jax
import jax.numpy as jnp
from jax import lax
from jax.experimental import pallas as pl
from jax.experimental.pallas import tpu as pltpu

D_MODEL = 1024
D_CONV = 512
CONV_WIDTH = 31
N_HEADS = 8
HEAD_DIM = 64
N_KV = 2
HPG = N_HEADS // N_KV
D_ATTN = N_HEADS * HEAD_DIM
N_BRANCH = 3
CMP_BLOCK = 32
CMP_STRIDE = 16
CMP_HIDDEN = 256
SEL_BLOCK = 64
SEL_SHIFT = 6
SEL_TOP = 16
WINDOW = 512
D_FF = 4 * D_MODEL
EPS = 1e-6
NEG = -1e30
FORCE = 1e30
MASKED = -1e32

LANES = 128
SUBLANES = 8
F32 = jnp.float32
BF16 = jnp.bfloat16

TM_PROJ = 512
CONV_HALO = 32
CONV_ROWS = 32
TM_FFN = 512
SEL_TILE = 512
Q_TILE = 256
ROW_CHUNK = 32
RANK_WAYS = 2
F_WIN_BLK = HEAD_DIM
F_WIN_OFF = HEAD_DIM + 1
F_WIN_ONE = HEAD_DIM + 2
F_CMP_IDX = HEAD_DIM + 3
WIN_KEYS = WINDOW + Q_TILE
VMEM_LIMIT = 56 * 1024 * 1024

C_U = 0
C_Q = 2 * D_CONV
C_KVC = C_Q + D_ATTN
C_KV = C_KVC + 4 * HEAD_DIM
C_GATE = C_KV + 8 * HEAD_DIM
C_END = C_GATE + LANES

NT_DIMS = (((1,), (1,)), ((), ()))


def _dot(a, b):
    return jnp.dot(a, b, preferred_element_type=F32)


def _dot_nt(a, b):
    return lax.dot_general(a, b, NT_DIMS, preferred_element_type=F32)


def _proj_in_kernel(x_ref, g_ref, w_ref, cw_ref, cb_ref, cg_ref, cbeta_ref,
                    conv_ref, q_ref, kvc_ref, k_ref, v_ref, gate_ref, ext_ref, sh_ref, *, seq):
    i = pl.program_id(0)

    @pl.when(i == 0)
    def _():
        ext_ref[...] = jnp.zeros(ext_ref.shape, F32)

    x = x_ref[...]
    ms = jnp.mean(x * x, axis=-1, keepdims=True)
    h = (x * lax.rsqrt(ms + EPS) * g_ref[...]).astype(BF16)

    zu = _dot(h, w_ref[:, C_U:C_Q])
    u = zu[:, :D_CONV] * jax.nn.sigmoid(zu[:, D_CONV:])

    tm = x.shape[0]
    pad64 = jnp.zeros((tm, HEAD_DIM), F32)

    def emit_q():
        zq = _dot(h, w_ref[:, C_Q:C_KVC])
        for hd in range(N_HEADS):
            q_ref[:, hd * LANES:(hd + 1) * LANES] = jnp.concatenate(
                [zq[:, hd * HEAD_DIM:(hd + 1) * HEAD_DIM], pad64], axis=1).astype(BF16)

    def emit_kvc():
        zc = _dot(h, w_ref[:, C_KVC:C_KV])
        for n in range(4):
            kvc_ref[n] = zc[:, n * HEAD_DIM:(n + 1) * HEAD_DIM]

    def emit_kv():
        t = (i * tm) % seq + lax.broadcasted_iota(jnp.int32, (tm, LANES), 0)
        lane = lax.broadcasted_iota(jnp.int32, (tm, LANES), 1)
        feat = jnp.where(lane == LANES - 1, (t & (SEL_BLOCK - 1)).astype(F32),
                         jnp.where(lane - HEAD_DIM == (t >> SEL_SHIFT), 1.0, 0.0))
        win_feat = jnp.where(lane == F_WIN_BLK, (t >> SEL_SHIFT).astype(F32),
                             jnp.where(lane == F_WIN_OFF, (t & (SEL_BLOCK - 1)).astype(F32),
                                       jnp.where(lane == F_WIN_ONE, 1.0, 0.0)))
        ones64 = jnp.ones((tm, HEAD_DIM), F32)
        zkv = _dot(h, w_ref[:, C_KV:C_GATE])
        for n in range(8):
            z = zkv[:, n * HEAD_DIM:(n + 1) * HEAD_DIM]
            kind, grp = divmod(n, N_KV)
            if kind % 2 == 0:
                k = jnp.concatenate([z, pad64], axis=1) + (feat if kind == 0 else win_feat)
                k_ref[kind + grp] = k.astype(BF16)
            else:
                v_ref[kind - 1 + grp] = jnp.concatenate([z, ones64, ones64, z], axis=1).astype(BF16)

    emit_q()
    emit_kvc()
    emit_kv()
    gate_ref[...] = jax.nn.sigmoid(_dot(h, w_ref[:, C_GATE:C_END]))
    _conv_tile(u, (i * tm) % seq == 0, cw_ref, cb_ref, cg_ref, cbeta_ref, conv_ref,
               ext_ref, sh_ref)


def _proj_in(x2, g1, w_a, conv_w, conv_b, cln_g, cln_b, seq):
    n_tok = x2.shape[0]
    grid = (n_tok // TM_PROJ,)
    row = lambda width: pl.BlockSpec((1, width), lambda i: (0, 0))
    return pl.pallas_call(
        functools.partial(_proj_in_kernel, seq=seq),
        grid=grid,
        in_specs=[
            pl.BlockSpec((TM_PROJ, D_MODEL), lambda i: (i, 0)),
            row(D_MODEL),
            pl.BlockSpec((D_MODEL, C_END), lambda i: (0, 0), pipeline_mode=pl.Buffered(1)),
            pl.BlockSpec((CONV_WIDTH * SUBLANES, D_CONV), lambda i: (0, 0)),
            row(D_CONV), row(D_CONV), row(D_CONV),
        ],
        out_specs=[
            pl.BlockSpec((TM_PROJ, D_CONV), lambda i: (i, 0)),
            pl.BlockSpec((TM_PROJ, N_HEADS * LANES), lambda i: (i, 0)),
            pl.BlockSpec((4, TM_PROJ, HEAD_DIM), lambda i: (0, i, 0)),
            pl.BlockSpec((4, TM_PROJ, LANES), lambda i: (0, i, 0)),
            pl.BlockSpec((4, TM_PROJ, 2 * LANES), lambda i: (0, i, 0)),
            pl.BlockSpec((TM_PROJ, LANES), lambda i: (i, 0)),
        ],
        out_shape=[
            jax.ShapeDtypeStruct((n_tok, D_CONV), BF16),
            jax.ShapeDtypeStruct((n_tok, N_HEADS * LANES), BF16),
            jax.ShapeDtypeStruct((4, n_tok, HEAD_DIM), F32),
            jax.ShapeDtypeStruct((4, n_tok, LANES), BF16),
            jax.ShapeDtypeStruct((4, n_tok, 2 * LANES), BF16),
            jax.ShapeDtypeStruct((n_tok, LANES), F32),
        ],
        scratch_shapes=[
            pltpu.VMEM((TM_PROJ + CONV_HALO, D_CONV), F32),
            pltpu.VMEM((SUBLANES - 1, TM_PROJ + CONV_HALO - SUBLANES, D_CONV), F32),
        ],
        compiler_params=pltpu.CompilerParams(
            dimension_semantics=("arbitrary",), vmem_limit_bytes=VMEM_LIMIT),
        name="proj_in",
    )(x2, g1, w_a, conv_w, conv_b, cln_g, cln_b)


def _compress_kernel(a_ref, pe_ref, w1_ref, w2_ref, o_ref):
    n_rows = a_ref.shape[2] // CMP_STRIDE
    a = jnp.concatenate(
        [a_ref[0, 0, pl.ds(i, n_rows, stride=CMP_STRIDE), :] for i in range(CMP_STRIDE)],
        axis=1).astype(BF16)
    half = CMP_STRIDE * HEAD_DIM
    top = _dot(a, w1_ref[0, :half, :])
    bot = _dot(a, w1_ref[0, half:, :])
    bot = pltpu.roll(bot, n_rows - 1, 0)
    pe = _dot(pe_ref[0].astype(BF16), w1_ref[0])
    hid = jax.nn.gelu(top + bot + pe, approximate=True)
    out = _dot(hid.astype(BF16), w2_ref[0])
    row = lax.broadcasted_iota(jnp.int32, out.shape, 0)
    lane = lax.broadcasted_iota(jnp.int32, out.shape, 1)
    is_key = pl.program_id(0) < N_KV
    out = out + jnp.where((lane == F_CMP_IDX) & is_key, row.astype(F32), 0.0)
    o_ref[0, 0] = jnp.where(row == n_rows - 1, 0.0, out).astype(BF16)


def _compress(a4, pes, w1s, w2s):
    _, batch, seq, _ = a4.shape
    n_rows = seq // CMP_STRIDE
    return pl.pallas_call(
        _compress_kernel,
        grid=(4, batch),
        in_specs=[
            pl.BlockSpec((1, 1, seq, HEAD_DIM), lambda n, b: (n, b, 0, 0)),
            pl.BlockSpec((1, 1, CMP_BLOCK * HEAD_DIM), lambda n, b: (n // N_KV, 0, 0)),
            pl.BlockSpec((1, CMP_BLOCK * HEAD_DIM, CMP_HIDDEN), lambda n, b: (n // N_KV, 0, 0)),
            pl.BlockSpec((1, CMP_HIDDEN, LANES), lambda n, b: (n // N_KV, 0, 0)),
        ],
        out_specs=pl.BlockSpec((1, 1, n_rows, LANES), lambda n, b: (n, b, 0, 0)),
        out_shape=jax.ShapeDtypeStruct((4, batch, n_rows, LANES), BF16),
        compiler_params=pltpu.CompilerParams(
            dimension_semantics=("arbitrary", "arbitrary"), vmem_limit_bytes=VMEM_LIMIT),
        name="compress",
    )(a4, pes, w1s, w2s)


def _conv_tile(u, first, w_ref, b_ref, g_ref, beta_ref, o_ref, ext_ref, sh_ref):
    tm = u.shape[0]
    ext_ref[0:CONV_HALO, :] = jnp.where(first, 0.0, ext_ref[tm:tm + CONV_HALO, :])
    ext_ref[CONV_HALO:, :] = u
    off = CONV_HALO - (CONV_WIDTH - 1)
    n_sh = sh_ref.shape[1]
    for b in range(1, SUBLANES):
        sh_ref[b - 1] = ext_ref[b:b + n_sh, :]
    for r in range(tm // CONV_ROWS):
        r0 = r * CONV_ROWS
        acc = jnp.zeros((CONV_ROWS, D_CONV), F32) + b_ref[...]
        for k in range(CONV_WIDTH):
            a, b = divmod(off + k, SUBLANES)
            lo = r0 + a * SUBLANES
            tap = ext_ref[lo:lo + CONV_ROWS, :] if b == 0 else sh_ref[b - 1, lo:lo + CONV_ROWS, :]
            w_k = w_ref[k * SUBLANES:(k + 1) * SUBLANES, :]
            acc = acc + tap * jnp.concatenate([w_k] * (CONV_ROWS // SUBLANES), axis=0)
        mu = jnp.mean(acc, axis=-1, keepdims=True)
        d = acc - mu
        var = jnp.mean(d * d, axis=-1, keepdims=True)
        y = d * lax.rsqrt(var + EPS) * g_ref[...] + beta_ref[...]
        o_ref[r0:r0 + CONV_ROWS, :] = (y * jax.nn.sigmoid(y)).astype(BF16)


def _exp_rows(s):
    e = jnp.exp(s - jnp.max(s, axis=-1, keepdims=True))
    return e, 1.0 / jnp.sum(e, axis=-1, keepdims=True)


def _attn_kernel(q_ref, kc_ref, vc_ref, ksel_ref, vsel_ref, kwin_ref, vwin_ref,
                 gate_ref, ovt_ref, gexp_ref, o_ref,
                 qa_ref, m_ref, acc_ref, s0_ref, s1_ref, p0_ref, p1_ref,
                 sw_ref, ew_ref, sc_ref, ec_ref):
    g = pl.program_id(1)
    step = pl.program_id(2)
    q0 = step * Q_TILE
    ci0 = step * (Q_TILE // SEL_BLOCK)
    rows = HPG * Q_TILE
    qb = q_ref[...]
    q_heads = [qb[:, h * LANES:(h + 1) * LANES] for h in range(HPG)]
    slope0 = jnp.where(g == 0, 0.5, 0.5 ** (HPG + 1)).astype(F32)
    slopes = [slope0 * (0.5 ** h) for h in range(HPG)]
    head_rows = lambda a, h: a[h * Q_TILE:(h + 1) * Q_TILE]

    lane = lax.broadcasted_iota(jnp.int32, (Q_TILE, LANES), 1)
    n_rc = Q_TILE // ROW_CHUNK


    win_blk0 = jnp.maximum(ci0 - WINDOW // SEL_BLOCK, 0)
    start = pl.multiple_of(win_blk0 * SEL_BLOCK, SEL_BLOCK)
    q_cw = []
    for h in range(HPG):
        feat = jnp.where(
            lane == F_WIN_BLK, SEL_BLOCK * slopes[h],
            jnp.where(lane == F_WIN_OFF, slopes[h],
                      jnp.where(lane == F_WIN_ONE, -SEL_BLOCK * slopes[h] * win_blk0.astype(F32),
                                jnp.where(lane == F_CMP_IDX, CMP_STRIDE * slopes[h], 0.0))))
        q_cw.append(jnp.where(lane < HEAD_DIM, q_heads[h].astype(F32), feat).astype(BF16))
    q_cw = jnp.concatenate(q_cw, axis=0)

    n_cmp = kc_ref.shape[2]
    sc_ref[...] = _dot_nt(q_cw, kc_ref[0, 0])
    r_cmp = [[None] * n_rc for _ in range(HPG)]
    p_sum = []
    for c in range(n_rc):
        rq = c * ROW_CHUNK + lax.broadcasted_iota(jnp.int32, (ROW_CHUNK, n_cmp), 0)
        cc = lax.broadcasted_iota(jnp.int32, (ROW_CHUNK, n_cmp), 1)
        valid = q0 + rq - cc * CMP_STRIDE >= CMP_BLOCK - 1
        p_chunk = jnp.zeros((ROW_CHUNK, n_cmp), F32)
        for h in range(HPG):
            r0 = h * Q_TILE + c * ROW_CHUNK
            e, r_sum = _exp_rows(jnp.where(valid, sc_ref[r0:r0 + ROW_CHUNK, :], NEG))
            p_chunk = p_chunk + e * r_sum
            ec_ref[r0:r0 + ROW_CHUNK, :] = e.astype(BF16)
            r_cmp[h][c] = r_sum
        p_sum.append(p_chunk)
    p_sum = jnp.concatenate(p_sum, axis=0)
    o_cmp_all = _dot(ec_ref[...], vc_ref[0, 0])

    p_hi = p_sum.astype(BF16)
    p_lo = (p_sum - p_hi.astype(F32)).astype(BF16)
    ovt = ovt_ref[...]
    imp = _dot_nt(ovt, p_hi) + _dot_nt(ovt, p_lo)
    n_sel = imp.shape[0]
    jrow = lax.broadcasted_iota(jnp.int32, (n_sel, Q_TILE), 0)
    qlane = lax.broadcasted_iota(jnp.int32, (n_sel, Q_TILE), 1)
    cl = ci0 + (qlane >> SEL_SHIFT)
    forced = (jrow == 0) | (jrow == cl) | (jrow == cl - 1)
    imp = jnp.where(forced, FORCE, jnp.where(jrow > cl, NEG, imp))
    sw_ref[...] = _dot_nt(q_cw, kwin_ref[0, pl.ds(start, WIN_KEYS), :])
    groups = [imp[8 * v:8 * v + 8, :] for v in range(n_sel // 8)]
    sub = lax.broadcasted_iota(jnp.int32, (8, Q_TILE), 0)
    ranks = [[jnp.zeros((8, Q_TILE), jnp.int32) for _ in range(RANK_WAYS)] for _ in groups]
    for jp in range(n_sel):
        row = groups[jp // 8][jp % 8:jp % 8 + 1, :]
        w = jp % RANK_WAYS
        for v in range(n_sel // 8):
            x = groups[v]
            if 8 * v + 7 < jp:
                ranks[v][w] = ranks[v][w] + jnp.where(row > x, 1, 0)
            elif 8 * v > jp:
                ranks[v][w] = ranks[v][w] + jnp.where(row >= x, 1, 0)
            else:
                tie = jnp.where(sub + 8 * v > jp, 1, 0)
                ranks[v][w] = (ranks[v][w] + jnp.where(row > x, 1, 0)
                               + jnp.where(row == x, tie, 0))
    rank = jnp.concatenate([sum(r[1:], r[0]) for r in ranks], axis=0)
    sel_t = jnp.where((rank < SEL_TOP) & (jrow <= cl), 1.0, 0.0)
    sel_t = jnp.concatenate([jnp.zeros((LANES - n_sel, Q_TILE), F32), sel_t], axis=0)
    sel_q = jnp.concatenate(
        [sel_t[:, c * LANES:(c + 1) * LANES].T for c in range(Q_TILE // LANES)], axis=0)
    qrow = lax.broadcasted_iota(jnp.int32, (Q_TILE, LANES), 0)
    base = ((lane - HEAD_DIM - ci0 - (qrow >> SEL_SHIFT)) * SEL_BLOCK).astype(F32)
    q_aug = []
    for h in range(HPG):
        feat = jnp.where(lane == LANES - 1, slopes[h],
                         jnp.where(sel_q > 0.5, slopes[h] * base, MASKED))
        q_aug.append(jnp.where(lane < HEAD_DIM, q_heads[h].astype(F32), feat).astype(BF16))
    q_aug = jnp.concatenate(q_aug, axis=0)

    def k_tile(kt):
        return ksel_ref[0, pl.ds(pl.multiple_of(kt * SEL_TILE, SEL_TILE), SEL_TILE), :]

    def v_tile(kt):
        return vsel_ref[0, pl.ds(pl.multiple_of(kt * SEL_TILE, SEL_TILE), SEL_TILE), :]

    s_bufs = (s0_ref, s1_ref)
    p_bufs = (p0_ref, p1_ref)
    qa_ref[...] = q_aug
    m_ref[...] = jnp.full((rows, LANES), NEG, F32)
    acc_ref[...] = jnp.zeros((rows, 2 * LANES), F32)
    p1_ref[...] = jnp.zeros((rows, SEL_TILE), BF16)
    s0_ref[...] = _dot_nt(q_aug, k_tile(0))

    o_win_all = []
    for pair in range(HPG // 2):
        for c in range(n_rc):
            rq = c * ROW_CHUNK + lax.broadcasted_iota(jnp.int32, (ROW_CHUNK, WIN_KEYS), 0)
            dw = q0 + rq - start - lax.broadcasted_iota(jnp.int32, (ROW_CHUNK, WIN_KEYS), 1)
            valid_w = (dw | (WINDOW - 1 - dw)) >= 0
            for h in (2 * pair, 2 * pair + 1):
                r0 = h * Q_TILE + c * ROW_CHUNK
                s = jnp.where(valid_w, sw_ref[r0:r0 + ROW_CHUNK, :], NEG)
                e = jnp.exp(s - jnp.max(s, axis=-1, keepdims=True))
                ew_ref[r0:r0 + ROW_CHUNK, :] = e.astype(BF16)
        pair_rows = slice(2 * pair * Q_TILE, (2 * pair + 2) * Q_TILE)
        o_win_all.append(_dot(ew_ref[pair_rows, :], vwin_ref[0, pl.ds(start, WIN_KEYS), :]))
    o_win_all = jnp.concatenate(o_win_all, axis=0)

    n_full = (ci0 + Q_TILE // SEL_BLOCK - 1) // (SEL_TILE // SEL_BLOCK)

    def online(s_ref, p_ref, pv, causal_tile):
        for c in range(n_rc):
            if causal_tile:
                kpos = n_full * SEL_TILE + lax.broadcasted_iota(
                    jnp.int32, (ROW_CHUNK, SEL_TILE), 1)
                tq = q0 + c * ROW_CHUNK + lax.broadcasted_iota(
                    jnp.int32, (ROW_CHUNK, SEL_TILE), 0)
                causal = kpos <= tq
            for h in range(HPG):
                r0 = h * Q_TILE + c * ROW_CHUNK
                rs = slice(r0, r0 + ROW_CHUNK)
                s = s_ref[rs, :]
                if causal_tile:
                    s = jnp.where(causal, s, MASKED)
                m_old = m_ref[rs, :]
                m_new = jnp.maximum(m_old, jnp.max(s, axis=-1, keepdims=True))
                m_ref[rs, :] = m_new
                p = [jnp.exp(s[:, j * LANES:(j + 1) * LANES] - m_new)
                     for j in range(SEL_TILE // LANES)]
                p_ref[rs, :] = jnp.concatenate(p, axis=1).astype(BF16)
                alpha = jnp.exp(m_old - m_new)
                acc_ref[rs, :] = jnp.concatenate([alpha, alpha], axis=1) * (acc_ref[rs, :]
                                                                           + pv[rs, :])

    def stage(i, par):
        s_bufs[1 - par][...] = _dot_nt(qa_ref[...], k_tile(i + 1))
        pv = _dot(p_bufs[1 - par][...], v_tile(jnp.maximum(i - 1, 0)))
        online(s_bufs[par], p_bufs[par], pv, False)

    def sweep(i, carry):
        lax.cond(i % 2 == 0, lambda: stage(i, 0), lambda: stage(i, 1))
        return carry

    lax.fori_loop(0, n_full, sweep, 0)

    def last(par):
        pv = _dot(p_bufs[1 - par][...], v_tile(jnp.maximum(n_full - 1, 0)))
        online(s_bufs[par], p_bufs[par], pv, True)
        acc_ref[...] += _dot(p_bufs[par][...], v_tile(n_full))

    lax.cond(n_full % 2 == 0, lambda: last(0), lambda: last(1))
    o_sel_all = acc_ref[...]

    gates = gate_ref[...]
    g_hi = gates.astype(BF16)
    g_lo = (gates - g_hi.astype(F32)).astype(BF16)
    expand = gexp_ref[...]
    g_wide = _dot(g_hi, expand) + _dot(g_lo, expand)
    even_lanes = lane < HEAD_DIM
    sees_cmp = q0 + lax.broadcasted_iota(jnp.int32, (Q_TILE, LANES), 0) >= CMP_BLOCK - 1
    pairs = []
    for pair in range(HPG // 2):
        h0, h1 = 2 * pair, 2 * pair + 1
        def gate(n, pair=pair):
            blk = n * (HPG // 2) + pair
            return g_wide[:, blk * LANES:(blk + 1) * LANES]
        def both(x0, x1):
            return jnp.where(even_lanes, x0, x1)
        o_cmp = both(head_rows(o_cmp_all, h0) * jnp.concatenate(r_cmp[h0], axis=0),
                     head_rows(o_cmp_all, h1) * jnp.concatenate(r_cmp[h1], axis=0))
        o_cmp = jnp.where(sees_cmp, o_cmp, 0.0)
        def normalised(acc_all):
            a0, a1 = head_rows(acc_all, h0), head_rows(acc_all, h1)
            return both(a0[:, :LANES], a1[:, LANES:]) * (1.0 / both(a0[:, LANES:], a1[:, :LANES]))
        pairs.append(gate(0) * o_cmp + gate(1) * normalised(o_sel_all)
                     + gate(2) * normalised(o_win_all))
    o_ref[...] = jnp.concatenate(pairs, axis=1).astype(BF16)


def _gate_expand():
    col = jnp.arange(N_KV * N_BRANCH * (HPG // 2) * LANES)
    blk, h_in_pair = col // LANES, (col % LANES) // HEAD_DIM
    g, n, pair = blk // (N_BRANCH * HPG // 2), (blk // (HPG // 2)) % N_BRANCH, blk % (HPG // 2)
    row = jnp.arange(LANES)[:, None]
    return (row == (n * N_HEADS + g * HPG + 2 * pair + h_in_pair)[None, :]).astype(BF16)


def _attention(q_pad, kvc, k4, v4, gates, ovt, batch, seq):
    n_steps = seq // Q_TILE
    rows = HPG * Q_TILE
    n_cmp = kvc.shape[2]
    n_sel = ovt.shape[0]
    k_spec = lambda branch: pl.BlockSpec(
        (1, seq, LANES), lambda b, g, c, branch=branch: (branch * N_KV + g, b, 0))
    v_spec = lambda branch: pl.BlockSpec(
        (1, seq, 2 * LANES), lambda b, g, c, branch=branch: (branch * N_KV + g, b, 0))
    return pl.pallas_call(
        _attn_kernel,
        grid=(batch, N_KV, n_steps),
        in_specs=[
            pl.BlockSpec((Q_TILE, HPG * LANES), lambda b, g, c: (b * n_steps + c, g)),
            pl.BlockSpec((1, 1, n_cmp, LANES), lambda b, g, c: (g, b, 0, 0)),
            pl.BlockSpec((1, 1, n_cmp, LANES), lambda b, g, c: (N_KV + g, b, 0, 0)),
            k_spec(0), v_spec(0), k_spec(1), v_spec(1),
            pl.BlockSpec((Q_TILE, LANES), lambda b, g, c: (b * n_steps + c, 0)),
            pl.BlockSpec((n_sel, n_cmp), lambda b, g, c: (0, 0)),
            pl.BlockSpec((LANES, N_BRANCH * (HPG // 2) * LANES), lambda b, g, c: (0, g)),
        ],
        out_specs=pl.BlockSpec((Q_TILE, HPG * HEAD_DIM), lambda b, g, c: (b * n_steps + c, g)),
        out_shape=jax.ShapeDtypeStruct((batch * seq, D_ATTN), BF16),
        scratch_shapes=[
            pltpu.VMEM((rows, LANES), BF16),
            pltpu.VMEM((rows, LANES), F32),
            pltpu.VMEM((rows, 2 * LANES), F32),
            pltpu.VMEM((rows, SEL_TILE), F32), pltpu.VMEM((rows, SEL_TILE), F32),
            pltpu.VMEM((rows, SEL_TILE), BF16), pltpu.VMEM((rows, SEL_TILE), BF16),
            pltpu.VMEM((rows, WIN_KEYS), F32), pltpu.VMEM((rows, WIN_KEYS), BF16),
            pltpu.VMEM((rows, n_cmp), F32), pltpu.VMEM((rows, n_cmp), BF16),
        ],
        compiler_params=pltpu.CompilerParams(
            dimension_semantics=("arbitrary", "arbitrary", "arbitrary"),
            vmem_limit_bytes=VMEM_LIMIT),
        name="attn",
    )(q_pad, kvc, kvc, k4, v4, k4, v4, gates, ovt, _gate_expand())


def _rms(x, g):
    return x * lax.rsqrt(jnp.mean(x * x, axis=-1, keepdims=True) + EPS) * g


def _out_ffn_kernel(x_ref, c_ref, a_ref, wc_ref, wa_ref, g2_ref, w1_ref, w2_ref, gf_ref, o_ref):
    x1 = x_ref[...] + _dot(c_ref[...], wc_ref[...]) + _dot(a_ref[...], wa_ref[...])
    h2 = _rms(x1, g2_ref[...]).astype(BF16)
    a = jnp.maximum(_dot(h2, w1_ref[...]), 0.0)
    y = x1 + _dot((a * a).astype(BF16), w2_ref[...])
    o_ref[...] = _rms(y, gf_ref[...])


def _out_ffn(x2, conv_out, attn_out, wc, wa, g2, w1, w2, gf):
    n_tok = x2.shape[0]
    const = lambda shape: pl.BlockSpec(shape, lambda i: (0, 0), pipeline_mode=pl.Buffered(1))
    return pl.pallas_call(
        _out_ffn_kernel,
        grid=(n_tok // TM_FFN,),
        in_specs=[
            pl.BlockSpec((TM_FFN, D_MODEL), lambda i: (i, 0)),
            pl.BlockSpec((TM_FFN, D_CONV), lambda i: (i, 0)),
            pl.BlockSpec((TM_FFN, D_ATTN), lambda i: (i, 0)),
            const((D_CONV, D_MODEL)),
            const((D_ATTN, D_MODEL)),
            const((1, D_MODEL)),
            const((D_MODEL, D_FF)),
            const((D_FF, D_MODEL)),
            const((1, D_MODEL)),
        ],
        out_specs=pl.BlockSpec((TM_FFN, D_MODEL), lambda i: (i, 0)),
        out_shape=jax.ShapeDtypeStruct((n_tok, D_MODEL), F32),
        compiler_params=pltpu.CompilerParams(
            dimension_semantics=("arbitrary",), vmem_limit_bytes=VMEM_LIMIT),
        name="out_ffn",
    )(x2, conv_out, attn_out, wc, wa, g2, w1, w2, gf)


def _pack_w_in(w_in):
    o2 = 2 * D_CONV
    o3 = o2 + D_ATTN
    o4 = o3 + 2 * N_BRANCH * N_KV * HEAD_DIM
    gate_w = w_in[:, o4:].reshape(D_MODEL, N_HEADS, N_BRANCH).transpose(0, 2, 1)
    gate_w = gate_w.reshape(D_MODEL, N_BRANCH * N_HEADS)
    cols = [w_in[:, :o2], w_in[:, o2:o3] * HEAD_DIM ** -0.5, w_in[:, o3:o4], gate_w,
            jnp.zeros((D_MODEL, LANES - N_BRANCH * N_HEADS), w_in.dtype)]
    return jnp.concatenate(cols, axis=1).astype(BF16)


def _overlap_t(n_cmp_rows, n_sel):
    c = jnp.arange(n_cmp_rows)[None, :]
    j = jnp.arange(n_sel)[:, None]
    ov = (c * CMP_STRIDE < (j + 1) * SEL_BLOCK) & (c * CMP_STRIDE + CMP_BLOCK > j * SEL_BLOCK)
    return ov.astype(BF16)


def kernel(x, norm1_g, w_in, dw_w, dw_b, cln_g, cln_b, ck_pe, ck_w1, ck_w2, cv_pe, cv_w1, cv_w2,
           w_out, norm2_g, w_ff1, w_ff2, norm_f_g):
    batch, seq, _ = x.shape
    n_tok = batch * seq
    x2 = x.reshape(n_tok, D_MODEL)
    assert norm1_g.shape[0] == 1, "the final norm is fused into the (single) layer's last kernel"
    for l in range(1):
        w_a = _pack_w_in(w_in[l])
        conv_w = jnp.repeat(dw_w[l][:, 0, :], SUBLANES, axis=0)
        conv_out, q_pad, kvc_raw, k4, v4, gates = _proj_in(
            x2, norm1_g[l][None, :], w_a, conv_w, dw_b[l][None, :], cln_g[l][None, :],
            cln_b[l][None, :], seq)

        n_rows = seq // CMP_STRIDE
        a4 = kvc_raw.reshape(4, batch, seq, HEAD_DIM)
        pes = jnp.stack([ck_pe[l], cv_pe[l]]).reshape(2, 1, CMP_BLOCK * HEAD_DIM)
        w1s = jnp.stack([ck_w1[l], cv_w1[l]]).astype(BF16)
        zero2 = jnp.zeros_like(ck_w2[l])
        w2s = jnp.stack([jnp.concatenate([ck_w2[l], zero2], axis=1),
                         jnp.concatenate([cv_w2[l], cv_w2[l]], axis=1)]).astype(BF16)
        kvc = _compress(a4, pes, w1s, w2s)

        ovt = _overlap_t(n_rows, seq // SEL_BLOCK)
        attn_out = _attention(q_pad, kvc, k4, v4, gates, ovt, batch, seq)

        wo = w_out[l].astype(BF16)
        x2 = _out_ffn(x2, conv_out, attn_out, wo[:D_CONV], wo[D_CONV:],
                      norm2_g[l][None, :], w_ff1[l].astype(BF16), w_ff2[l].astype(BF16),
                      norm_f_g[None, :])
    return x2.reshape(batch, seq, D_MODEL)
```

```python
import functools

import jax
import jax.numpy as jnp
from jax import lax
from jax.experimental import pallas as pl
from jax.experimental.pallas import tpu as pltpu

D_MODEL = 1024
D_CONV = 512
CONV_WIDTH = 31
N_HEADS = 8
HEAD_DIM = 64
N_KV = 2
HPG = N_HEADS // N_KV
D_ATTN = N_HEADS * HEAD_DIM
N_BRANCH = 3
CMP_BLOCK = 32
CMP_STRIDE = 16
CMP_HIDDEN = 256
SEL_BLOCK = 64
SEL_SHIFT = 6
SEL_TOP = 16
WINDOW = 512
D_FF = 4 * D_MODEL
EPS = 1e-6
NEG = -1e30
FORCE = 1e30
MASKED = -1e32

LANES = 128
SUBLANES = 8
F32 = jnp.float32
BF16 = jnp.bfloat16

TM_PROJ = 512
CONV_HALO = 32
CONV_ROWS = 32
TM_FFN = 512
SEL_TILE = 512
Q_TILE = 256
ROW_CHUNK = 32
RANK_WAYS = 2
F_WIN_BLK = HEAD_DIM
F_WIN_OFF = HEAD_DIM + 1
F_WIN_ONE = HEAD_DIM + 2
F_CMP_IDX = HEAD_DIM + 3
WIN_KEYS = WINDOW + Q_TILE
VMEM_LIMIT = 56 * 1024 * 1024

C_U = 0
C_Q = 2 * D_CONV
C_KVC = C_Q + D_ATTN
C_KV = C_KVC + 4 * HEAD_DIM
C_GATE = C_KV + 8 * HEAD_DIM
C_END = C_GATE + LANES

NT_DIMS = (((1,), (1,)), ((), ()))


def _dot(a, b):
    return jnp.dot(a, b, preferred_element_type=F32)


def _dot_nt(a, b):
    return lax.dot_general(a, b, NT_DIMS, preferred_element_type=F32)


def _proj_in_kernel(x_ref, g_ref, w_ref, cw_ref, cb_ref, cg_ref, cbeta_ref,
                    conv_ref, q_ref, kvc_ref, k_ref, v_ref, gate_ref, ext_ref, sh_ref, *, seq):
    i = pl.program_id(0)

    @pl.when(i == 0)
    def _():
        ext_ref[...] = jnp.zeros(ext_ref.shape, F32)

    x = x_ref[...]
    ms = jnp.mean(x * x, axis=-1, keepdims=True)
    h = (x * lax.rsqrt(ms + EPS) * g_ref[...]).astype(BF16)

    zu = _dot(h, w_ref[:, C_U:C_Q])
    u = zu[:, :D_CONV] * jax.nn.sigmoid(zu[:, D_CONV:])

    tm = x.shape[0]
    pad64 = jnp.zeros((tm, HEAD_DIM), F32)

    def emit_q():
        zq = _dot(h, w_ref[:, C_Q:C_KVC])
        for hd in range(N_HEADS):
            q_ref[:, hd * LANES:(hd + 1) * LANES] = jnp.concatenate(
                [zq[:, hd * HEAD_DIM:(hd + 1) * HEAD_DIM], pad64], axis=1).astype(BF16)

    def emit_kvc():
        zc = _dot(h, w_ref[:, C_KVC:C_KV])
        for n in range(4):
            kvc_ref[n] = zc[:, n * HEAD_DIM:(n + 1) * HEAD_DIM]

    def emit_kv():
        t = (i * tm) % seq + lax.broadcasted_iota(jnp.int32, (tm, LANES), 0)
        lane = lax.broadcasted_iota(jnp.int32, (tm, LANES), 1)
        feat = jnp.where(lane == LANES - 1, (t & (SEL_BLOCK - 1)).astype(F32),
                         jnp.where(lane - HEAD_DIM == (t >> SEL_SHIFT), 1.0, 0.0))
        win_feat = jnp.where(lane == F_WIN_BLK, (t >> SEL_SHIFT).astype(F32),
                             jnp.where(lane == F_WIN_OFF, (t & (SEL_BLOCK - 1)).astype(F32),
                                       jnp.where(lane == F_WIN_ONE, 1.0, 0.0)))
        ones64 = jnp.ones((tm, HEAD_DIM), F32)
        zkv = _dot(h, w_ref[:, C_KV:C_GATE])
        for n in range(8):
            z = zkv[:, n * HEAD_DIM:(n + 1) * HEAD_DIM]
            kind, grp = divmod(n, N_KV)
            if kind % 2 == 0:
                k = jnp.concatenate([z, pad64], axis=1) + (feat if kind == 0 else win_feat)
                k_ref[kind + grp] = k.astype(BF16)
            else:
                v_ref[kind - 1 + grp] = jnp.concatenate([z, ones64, ones64, z], axis=1).astype(BF16)

    emit_q()
    emit_kvc()
    emit_kv()
    gate_ref[...] = jax.nn.sigmoid(_dot(h, w_ref[:, C_GATE:C_END]))
    _conv_tile(u, (i * tm) % seq == 0, cw_ref, cb_ref, cg_ref, cbeta_ref, conv_ref,
               ext_ref, sh_ref)


def _proj_in(x2, g1, w_a, conv_w, conv_b, cln_g, cln_b, seq):
    n_tok = x2.shape[0]
    grid = (n_tok // TM_PROJ,)
    row = lambda width: pl.BlockSpec((1, width), lambda i: (0, 0))
    return pl.pallas_call(
        functools.partial(_proj_in_kernel, seq=seq),
        grid=grid,
        in_specs=[
            pl.BlockSpec((TM_PROJ, D_MODEL), lambda i: (i, 0)),
            row(D_MODEL),
            pl.BlockSpec((D_MODEL, C_END), lambda i: (0, 0), pipeline_mode=pl.Buffered(1)),
            pl.BlockSpec((CONV_WIDTH * SUBLANES, D_CONV), lambda i: (0, 0)),
            row(D_CONV), row(D_CONV), row(D_CONV),
        ],
        out_specs=[
            pl.BlockSpec((TM_PROJ, D_CONV), lambda i: (i, 0)),
            pl.BlockSpec((TM_PROJ, N_HEADS * LANES), lambda i: (i, 0)),
            pl.BlockSpec((4, TM_PROJ, HEAD_DIM), lambda i: (0, i, 0)),
            pl.BlockSpec((4, TM_PROJ, LANES), lambda i: (0, i, 0)),
            pl.BlockSpec((4, TM_PROJ, 2 * LANES), lambda i: (0, i, 0)),
            pl.BlockSpec((TM_PROJ, LANES), lambda i: (i, 0)),
        ],
        out_shape=[
            jax.ShapeDtypeStruct((n_tok, D_CONV), BF16),
            jax.ShapeDtypeStruct((n_tok, N_HEADS * LANES), BF16),
            jax.ShapeDtypeStruct((4, n_tok, HEAD_DIM), F32),
            jax.ShapeDtypeStruct((4, n_tok, LANES), BF16),
            jax.ShapeDtypeStruct((4, n_tok, 2 * LANES), BF16),
            jax.ShapeDtypeStruct((n_tok, LANES), F32),
        ],
        scratch_shapes=[
            pltpu.VMEM((TM_PROJ + CONV_HALO, D_CONV), F32),
            pltpu.VMEM((SUBLANES - 1, TM_PROJ + CONV_HALO - SUBLANES, D_CONV), F32),
        ],
        compiler_params=pltpu.CompilerParams(
            dimension_semantics=("arbitrary",), vmem_limit_bytes=VMEM_LIMIT),
        name="proj_in",
    )(x2, g1, w_a, conv_w, conv_b, cln_g, cln_b)


def _compress_kernel(a_ref, pe_ref, w1_ref, w2_ref, o_ref):
    n_rows = a_ref.shape[2] // CMP_STRIDE
    a = jnp.concatenate(
        [a_ref[0, 0, pl.ds(i, n_rows, stride=CMP_STRIDE), :] for i in range(CMP_STRIDE)],
        axis=1).astype(BF16)
    half = CMP_STRIDE * HEAD_DIM
    top = _dot(a, w1_ref[0, :half, :])
    bot = _dot(a, w1_ref[0, half:, :])
    bot = pltpu.roll(bot, n_rows - 1, 0)
    pe = _dot(pe_ref[0].astype(BF16), w1_ref[0])
    hid = jax.nn.gelu(top + bot + pe, approximate=True)
    out = _dot(hid.astype(BF16), w2_ref[0])
    row = lax.broadcasted_iota(jnp.int32, out.shape, 0)
    lane = lax.broadcasted_iota(jnp.int32, out.shape, 1)
    is_key = pl.program_id(0) < N_KV
    out = out + jnp.where((lane == F_CMP_IDX) & is_key, row.astype(F32), 0.0)
    o_ref[0, 0] = jnp.where(row == n_rows - 1, 0.0, out).astype(BF16)


def _compress(a4, pes, w1s, w2s):
    _, batch, seq, _ = a4.shape
    n_rows = seq // CMP_STRIDE
    return pl.pallas_call(
        _compress_kernel,
        grid=(4, batch),
        in_specs=[
            pl.BlockSpec((1, 1, seq, HEAD_DIM), lambda n, b: (n, b, 0, 0)),
            pl.BlockSpec((1, 1, CMP_BLOCK * HEAD_DIM), lambda n, b: (n // N_KV, 0, 0)),
            pl.BlockSpec((1, CMP_BLOCK * HEAD_DIM, CMP_HIDDEN), lambda n, b: (n // N_KV, 0, 0)),
            pl.BlockSpec((1, CMP_HIDDEN, LANES), lambda n, b: (n // N_KV, 0, 0)),
        ],
        out_specs=pl.BlockSpec((1, 1, n_rows, LANES), lambda n, b: (n, b, 0, 0)),
        out_shape=jax.ShapeDtypeStruct((4, batch, n_rows, LANES), BF16),
        compiler_params=pltpu.CompilerParams(
            dimension_semantics=("arbitrary", "arbitrary"), vmem_limit_bytes=VMEM_LIMIT),
        name="compress",
    )(a4, pes, w1s, w2s)


def _conv_tile(u, first, w_ref, b_ref, g_ref, beta_ref, o_ref, ext_ref, sh_ref):
    tm = u.shape[0]
    ext_ref[0:CONV_HALO, :] = jnp.where(first, 0.0, ext_ref[tm:tm + CONV_HALO, :])
    ext_ref[CONV_HALO:, :] = u
    off = CONV_HALO - (CONV_WIDTH - 1)
    n_sh = sh_ref.shape[1]
    for b in range(1, SUBLANES):
        sh_ref[b - 1] = ext_ref[b:b + n_sh, :]
    for r in range(tm // CONV_ROWS):
        r0 = r * CONV_ROWS
        acc = jnp.zeros((CONV_ROWS, D_CONV), F32) + b_ref[...]
        for k in range(CONV_WIDTH):
            a, b = divmod(off + k, SUBLANES)
            lo = r0 + a * SUBLANES
            tap = ext_ref[lo:lo + CONV_ROWS, :] if b == 0 else sh_ref[b - 1, lo:lo + CONV_ROWS, :]
            w_k = w_ref[k * SUBLANES:(k + 1) * SUBLANES, :]
            acc = acc + tap * jnp.concatenate([w_k] * (CONV_ROWS // SUBLANES), axis=0)
        mu = jnp.mean(acc, axis=-1, keepdims=True)
        d = acc - mu
        var = jnp.mean(d * d, axis=-1, keepdims=True)
        y = d * lax.rsqrt(var + EPS) * g_ref[...] + beta_ref[...]
        o_ref[r0:r0 + CONV_ROWS, :] = (y * jax.nn.sigmoid(y)).astype(BF16)


def _exp_rows(s):
    e = jnp.exp(s - jnp.max(s, axis=-1, keepdims=True))
    return e, 1.0 / jnp.sum(e, axis=-1, keepdims=True)


def _attn_kernel(q_ref, kc_ref, vc_ref, ksel_ref, vsel_ref, kwin_ref, vwin_ref,
                 gate_ref, ovt_ref, gexp_ref, o_ref,
                 qa_ref, m_ref, acc_ref, s0_ref, s1_ref, p0_ref, p1_ref,
                 sw_ref, ew_ref, sc_ref, ec_ref, tiles_ref):
    g = pl.program_id(1)
    step = pl.program_id(2)
    q0 = step * Q_TILE
    ci0 = step * (Q_TILE // SEL_BLOCK)
    rows = HPG * Q_TILE
    qb = q_ref[...]
    q_heads = [qb[:, h * LANES:(h + 1) * LANES] for h in range(HPG)]
    slope0 = jnp.where(g == 0, 0.5, 0.5 ** (HPG + 1)).astype(F32)
    slopes = [slope0 * (0.5 ** h) for h in range(HPG)]
    head_rows = lambda a, h: a[h * Q_TILE:(h + 1) * Q_TILE]

    lane = lax.broadcasted_iota(jnp.int32, (Q_TILE, LANES), 1)
    n_rc = Q_TILE // ROW_CHUNK


    win_blk0 = jnp.maximum(ci0 - WINDOW // SEL_BLOCK, 0)
    start = pl.multiple_of(win_blk0 * SEL_BLOCK, SEL_BLOCK)
    q_cw = []
    for h in range(HPG):
        feat = jnp.where(
            lane == F_WIN_BLK, SEL_BLOCK * slopes[h],
            jnp.where(lane == F_WIN_OFF, slopes[h],
                      jnp.where(lane == F_WIN_ONE, -SEL_BLOCK * slopes[h] * win_blk0.astype(F32),
                                jnp.where(lane == F_CMP_IDX, CMP_STRIDE * slopes[h], 0.0))))
        q_cw.append(jnp.where(lane < HEAD_DIM, q_heads[h].astype(F32), feat).astype(BF16))
    q_cw = jnp.concatenate(q_cw, axis=0)

    n_cmp = kc_ref.shape[2]
    sc_ref[...] = _dot_nt(q_cw, kc_ref[0, 0])
    r_cmp = [[None] * n_rc for _ in range(HPG)]
    p_sum = []
    for c in range(n_rc):
        rq = c * ROW_CHUNK + lax.broadcasted_iota(jnp.int32, (ROW_CHUNK, n_cmp), 0)
        cc = lax.broadcasted_iota(jnp.int32, (ROW_CHUNK, n_cmp), 1)
        valid = q0 + rq - cc * CMP_STRIDE >= CMP_BLOCK - 1
        p_chunk = jnp.zeros((ROW_CHUNK, n_cmp), F32)
        for h in range(HPG):
            r0 = h * Q_TILE + c * ROW_CHUNK
            e, r_sum = _exp_rows(jnp.where(valid, sc_ref[r0:r0 + ROW_CHUNK, :], NEG))
            p_chunk = p_chunk + e * r_sum
            ec_ref[r0:r0 + ROW_CHUNK, :] = e.astype(BF16)
            r_cmp[h][c] = r_sum
        p_sum.append(p_chunk)
    p_sum = jnp.concatenate(p_sum, axis=0)
    o_cmp_all = _dot(ec_ref[...], vc_ref[0, 0])

    p_hi = p_sum.astype(BF16)
    p_lo = (p_sum - p_hi.astype(F32)).astype(BF16)
    ovt = ovt_ref[...]
    imp = _dot_nt(ovt, p_hi) + _dot_nt(ovt, p_lo)
    n_sel = imp.shape[0]
    jrow = lax.broadcasted_iota(jnp.int32, (n_sel, Q_TILE), 0)
    qlane = lax.broadcasted_iota(jnp.int32, (n_sel, Q_TILE), 1)
    cl = ci0 + (qlane >> SEL_SHIFT)
    forced = (jrow == 0) | (jrow == cl) | (jrow == cl - 1)
    imp = jnp.where(forced, FORCE, jnp.where(jrow > cl, NEG, imp))
    sw_ref[...] = _dot_nt(q_cw, kwin_ref[0, pl.ds(start, WIN_KEYS), :])
    groups = [imp[8 * v:8 * v + 8, :] for v in range(n_sel // 8)]
    sub = lax.broadcasted_iota(jnp.int32, (8, Q_TILE), 0)
    ranks = [[jnp.zeros((8, Q_TILE), jnp.int32) for _ in range(RANK_WAYS)] for _ in groups]
    for jp in range(n_sel):
        row = groups[jp // 8][jp % 8:jp % 8 + 1, :]
        w = jp % RANK_WAYS
        for v in range(n_sel // 8):
            x = groups[v]
            if 8 * v + 7 < jp:
                ranks[v][w] = ranks[v][w] + jnp.where(row > x, 1, 0)
            elif 8 * v > jp:
                ranks[v][w] = ranks[v][w] + jnp.where(row >= x, 1, 0)
            else:
                tie = jnp.where(sub + 8 * v > jp, 1, 0)
                ranks[v][w] = (ranks[v][w] + jnp.where(row > x, 1, 0)
                               + jnp.where(row == x, tie, 0))
    rank = jnp.concatenate([sum(r[1:], r[0]) for r in ranks], axis=0)
    sel_t = jnp.where((rank < SEL_TOP) & (jrow <= cl), 1.0, 0.0)

    tile_blocks = SEL_TILE // SEL_BLOCK
    n_full = (ci0 + Q_TILE // SEL_BLOCK - 1) // tile_blocks
    n_active = jnp.int32(0)
    for t in range(n_sel // tile_blocks):
        wanted = jnp.max(sel_t[t * tile_blocks:(t + 1) * tile_blocks, :]) > 0.5
        tiles_ref[n_active] = t
        n_active = n_active + jnp.where(wanted & (t < n_full), 1, 0)
    tiles_ref[n_active] = n_full

    sel_t = jnp.concatenate([jnp.zeros((LANES - n_sel, Q_TILE), F32), sel_t], axis=0)
    sel_q = jnp.concatenate(
        [sel_t[:, c * LANES:(c + 1) * LANES].T for c in range(Q_TILE // LANES)], axis=0)
    qrow = lax.broadcasted_iota(jnp.int32, (Q_TILE, LANES), 0)
    base = ((lane - HEAD_DIM - ci0 - (qrow >> SEL_SHIFT)) * SEL_BLOCK).astype(F32)
    q_aug = []
    for h in range(HPG):
        feat = jnp.where(lane == LANES - 1, slopes[h],
                         jnp.where(sel_q > 0.5, slopes[h] * base, MASKED))
        q_aug.append(jnp.where(lane < HEAD_DIM, q_heads[h].astype(F32), feat).astype(BF16))
    q_aug = jnp.concatenate(q_aug, axis=0)

    def k_tile(kt):
        return ksel_ref[0, pl.ds(pl.multiple_of(kt * SEL_TILE, SEL_TILE), SEL_TILE), :]

    def v_tile(kt):
        return vsel_ref[0, pl.ds(pl.multiple_of(kt * SEL_TILE, SEL_TILE), SEL_TILE), :]

    s_bufs = (s0_ref, s1_ref)
    p_bufs = (p0_ref, p1_ref)
    qa_ref[...] = q_aug
    m_ref[...] = jnp.full((rows, LANES), NEG, F32)
    acc_ref[...] = jnp.zeros((rows, 2 * LANES), F32)
    p1_ref[...] = jnp.zeros((rows, SEL_TILE), BF16)
    s0_ref[...] = _dot_nt(q_aug, k_tile(tiles_ref[0]))

    o_win_all = []
    for pair in range(HPG // 2):
        for c in range(n_rc):
            rq = c * ROW_CHUNK + lax.broadcasted_iota(jnp.int32, (ROW_CHUNK, WIN_KEYS), 0)
            dw = q0 + rq - start - lax.broadcasted_iota(jnp.int32, (ROW_CHUNK, WIN_KEYS), 1)
            valid_w = (dw | (WINDOW - 1 - dw)) >= 0
            for h in (2 * pair, 2 * pair + 1):
                r0 = h * Q_TILE + c * ROW_CHUNK
                s = jnp.where(valid_w, sw_ref[r0:r0 + ROW_CHUNK, :], NEG)
                e = jnp.exp(s - jnp.max(s, axis=-1, keepdims=True))
                ew_ref[r0:r0 + ROW_CHUNK, :] = e.astype(BF16)
        pair_rows = slice(2 * pair * Q_TILE, (2 * pair + 2) * Q_TILE)
        o_win_all.append(_dot(ew_ref[pair_rows, :], vwin_ref[0, pl.ds(start, WIN_KEYS), :]))
    o_win_all = jnp.concatenate(o_win_all, axis=0)

    def online(s_ref, p_ref, pv, causal_tile):
        for c in range(n_rc):
            if causal_tile:
                kpos = n_full * SEL_TILE + lax.broadcasted_iota(
                    jnp.int32, (ROW_CHUNK, SEL_TILE), 1)
                tq = q0 + c * ROW_CHUNK + lax.broadcasted_iota(
                    jnp.int32, (ROW_CHUNK, SEL_TILE), 0)
                causal = kpos <= tq
            for h in range(HPG):
                r0 = h * Q_TILE + c * ROW_CHUNK
                rs = slice(r0, r0 + ROW_CHUNK)
                s = s_ref[rs, :]
                if causal_tile:
                    s = jnp.where(causal, s, MASKED)
                m_old = m_ref[rs, :]
                m_new = jnp.maximum(m_old, jnp.max(s, axis=-1, keepdims=True))
                m_ref[rs, :] = m_new
                p = [jnp.exp(s[:, j * LANES:(j + 1) * LANES] - m_new)
                     for j in range(SEL_TILE // LANES)]
                p_ref[rs, :] = jnp.concatenate(p, axis=1).astype(BF16)
                alpha = jnp.exp(m_old - m_new)
                acc_ref[rs, :] = jnp.concatenate([alpha, alpha], axis=1) * (acc_ref[rs, :]
                                                                           + pv[rs, :])

    def stage(i, par):
        s_bufs[1 - par][...] = _dot_nt(qa_ref[...], k_tile(tiles_ref[i + 1]))
        pv = _dot(p_bufs[1 - par][...], v_tile(tiles_ref[jnp.maximum(i - 1, 0)]))
        online(s_bufs[par], p_bufs[par], pv, False)

    def sweep(i, carry):
        lax.cond(i % 2 == 0, lambda: stage(i, 0), lambda: stage(i, 1))
        return carry

    lax.fori_loop(0, n_active, sweep, 0)

    def last(par):
        pv = _dot(p_bufs[1 - par][...], v_tile(tiles_ref[jnp.maximum(n_active - 1, 0)]))
        online(s_bufs[par], p_bufs[par], pv, True)
        acc_ref[...] += _dot(p_bufs[par][...], v_tile(n_full))

    lax.cond(n_active % 2 == 0, lambda: last(0), lambda: last(1))
    o_sel_all = acc_ref[...]

    gates = gate_ref[...]
    g_hi = gates.astype(BF16)
    g_lo = (gates - g_hi.astype(F32)).astype(BF16)
    expand = gexp_ref[...]
    g_wide = _dot(g_hi, expand) + _dot(g_lo, expand)
    even_lanes = lane < HEAD_DIM
    sees_cmp = q0 + lax.broadcasted_iota(jnp.int32, (Q_TILE, LANES), 0) >= CMP_BLOCK - 1
    pairs = []
    for pair in range(HPG // 2):
        h0, h1 = 2 * pair, 2 * pair + 1
        def gate(n, pair=pair):
            blk = n * (HPG // 2) + pair
            return g_wide[:, blk * LANES:(blk + 1) * LANES]
        def both(x0, x1):
            return jnp.where(even_lanes, x0, x1)
        o_cmp = both(head_rows(o_cmp_all, h0) * jnp.concatenate(r_cmp[h0], axis=0),
                     head_rows(o_cmp_all, h1) * jnp.concatenate(r_cmp[h1], axis=0))
        o_cmp = jnp.where(sees_cmp, o_cmp, 0.0)
        def normalised(acc_all):
            a0, a1 = head_rows(acc_all, h0), head_rows(acc_all, h1)
            return both(a0[:, :LANES], a1[:, LANES:]) * (1.0 / both(a0[:, LANES:], a1[:, :LANES]))
        pairs.append(gate(0) * o_cmp + gate(1) * normalised(o_sel_all)
                     + gate(2) * normalised(o_win_all))
    o_ref[...] = jnp.concatenate(pairs, axis=1).astype(BF16)


def _gate_expand():
    col = jnp.arange(N_KV * N_BRANCH * (HPG // 2) * LANES)
    blk, h_in_pair = col // LANES, (col % LANES) // HEAD_DIM
    g, n, pair = blk // (N_BRANCH * HPG // 2), (blk // (HPG // 2)) % N_BRANCH, blk % (HPG // 2)
    row = jnp.arange(LANES)[:, None]
    return (row == (n * N_HEADS + g * HPG + 2 * pair + h_in_pair)[None, :]).astype(BF16)


def _attention(q_pad, kvc, k4, v4, gates, ovt, batch, seq):
    n_steps = seq // Q_TILE
    rows = HPG * Q_TILE
    n_cmp = kvc.shape[2]
    n_sel = ovt.shape[0]
    k_spec = lambda branch: pl.BlockSpec(
        (1, seq, LANES), lambda b, g, c, branch=branch: (branch * N_KV + g, b, 0))
    v_spec = lambda branch: pl.BlockSpec(
        (1, seq, 2 * LANES), lambda b, g, c, branch=branch: (branch * N_KV + g, b, 0))
    return pl.pallas_call(
        _attn_kernel,
        grid=(batch, N_KV, n_steps),
        in_specs=[
            pl.BlockSpec((Q_TILE, HPG * LANES), lambda b, g, c: (b * n_steps + c, g)),
            pl.BlockSpec((1, 1, n_cmp, LANES), lambda b, g, c: (g, b, 0, 0)),
            pl.BlockSpec((1, 1, n_cmp, LANES), lambda b, g, c: (N_KV + g, b, 0, 0)),
            k_spec(0), v_spec(0), k_spec(1), v_spec(1),
            pl.BlockSpec((Q_TILE, LANES), lambda b, g, c: (b * n_steps + c, 0)),
            pl.BlockSpec((n_sel, n_cmp), lambda b, g, c: (0, 0)),
            pl.BlockSpec((LANES, N_BRANCH * (HPG // 2) * LANES), lambda b, g, c: (0, g)),
        ],
        out_specs=pl.BlockSpec((Q_TILE, HPG * HEAD_DIM), lambda b, g, c: (b * n_steps + c, g)),
        out_shape=jax.ShapeDtypeStruct((batch * seq, D_ATTN), BF16),
        scratch_shapes=[
            pltpu.VMEM((rows, LANES), BF16),
            pltpu.VMEM((rows, LANES), F32),
            pltpu.VMEM((rows, 2 * LANES), F32),
            pltpu.VMEM((rows, SEL_TILE), F32), pltpu.VMEM((rows, SEL_TILE), F32),
            pltpu.VMEM((rows, SEL_TILE), BF16), pltpu.VMEM((rows, SEL_TILE), BF16),
            pltpu.VMEM((rows, WIN_KEYS), F32), pltpu.VMEM((rows, WIN_KEYS), BF16),
            pltpu.VMEM((rows, n_cmp), F32), pltpu.VMEM((rows, n_cmp), BF16),
            pltpu.SMEM((seq // SEL_TILE + 1,), jnp.int32),
        ],
        compiler_params=pltpu.CompilerParams(
            dimension_semantics=("arbitrary", "arbitrary", "arbitrary"),
            vmem_limit_bytes=VMEM_LIMIT),
        name="attn",
    )(q_pad, kvc, kvc, k4, v4, k4, v4, gates, ovt, _gate_expand())


def _rms(x, g):
    return x * lax.rsqrt(jnp.mean(x * x, axis=-1, keepdims=True) + EPS) * g


def _out_ffn_kernel(x_ref, c_ref, a_ref, wc_ref, wa_ref, g2_ref, w1_ref, w2_ref, gf_ref, o_ref):
    x1 = x_ref[...] + _dot(c_ref[...], wc_ref[...]) + _dot(a_ref[...], wa_ref[...])
    h2 = _rms(x1, g2_ref[...]).astype(BF16)
    a = jnp.maximum(_dot(h2, w1_ref[...]), 0.0)
    y = x1 + _dot((a * a).astype(BF16), w2_ref[...])
    o_ref[...] = _rms(y, gf_ref[...])


def _out_ffn(x2, conv_out, attn_out, wc, wa, g2, w1, w2, gf):
    n_tok = x2.shape[0]
    const = lambda shape: pl.BlockSpec(shape, lambda i: (0, 0), pipeline_mode=pl.Buffered(1))
    return pl.pallas_call(
        _out_ffn_kernel,
        grid=(n_tok // TM_FFN,),
        in_specs=[
            pl.BlockSpec((TM_FFN, D_MODEL), lambda i: (i, 0)),
            pl.BlockSpec((TM_FFN, D_CONV), lambda i: (i, 0)),
            pl.BlockSpec((TM_FFN, D_ATTN), lambda i: (i, 0)),
            const((D_CONV, D_MODEL)),
            const((D_ATTN, D_MODEL)),
            const((1, D_MODEL)),
            const((D_MODEL, D_FF)),
            const((D_FF, D_MODEL)),
            const((1, D_MODEL)),
        ],
        out_specs=pl.BlockSpec((TM_FFN, D_MODEL), lambda i: (i, 0)),
        out_shape=jax.ShapeDtypeStruct((n_tok, D_MODEL), F32),
        compiler_params=pltpu.CompilerParams(
            dimension_semantics=("arbitrary",), vmem_limit_bytes=VMEM_LIMIT),
        name="out_ffn",
    )(x2, conv_out, attn_out, wc, wa, g2, w1, w2, gf)


def _pack_w_in(w_in):
    o2 = 2 * D_CONV
    o3 = o2 + D_ATTN
    o4 = o3 + 2 * N_BRANCH * N_KV * HEAD_DIM
    gate_w = w_in[:, o4:].reshape(D_MODEL, N_HEADS, N_BRANCH).transpose(0, 2, 1)
    gate_w = gate_w.reshape(D_MODEL, N_BRANCH * N_HEADS)
    cols = [w_in[:, :o2], w_in[:, o2:o3] * HEAD_DIM ** -0.5, w_in[:, o3:o4], gate_w,
            jnp.zeros((D_MODEL, LANES - N_BRANCH * N_HEADS), w_in.dtype)]
    return jnp.concatenate(cols, axis=1).astype(BF16)


def _overlap_t(n_cmp_rows, n_sel):
    c = jnp.arange(n_cmp_rows)[None, :]
    j = jnp.arange(n_sel)[:, None]
    ov = (c * CMP_STRIDE < (j + 1) * SEL_BLOCK) & (c * CMP_STRIDE + CMP_BLOCK > j * SEL_BLOCK)
    return ov.astype(BF16)


def kernel(x, norm1_g, w_in, dw_w, dw_b, cln_g, cln_b, ck_pe, ck_w1, ck_w2, cv_pe, cv_w1, cv_w2,
           w_out, norm2_g, w_ff1, w_ff2, norm_f_g):
    batch, seq, _ = x.shape
    n_tok = batch * seq
    x2 = x.reshape(n_tok, D_MODEL)
    assert norm1_g.shape[0] == 1, "the final norm is fused into the (single) layer's last kernel"
    for l in range(1):
        w_a = _pack_w_in(w_in[l])
        conv_w = jnp.repeat(dw_w[l][:, 0, :], SUBLANES, axis=0)
        conv_out, q_pad, kvc_raw, k4, v4, gates = _proj_in(
            x2, norm1_g[l][None, :], w_a, conv_w, dw_b[l][None, :], cln_g[l][None, :],
            cln_b[l][None, :], seq)

        n_rows = seq // CMP_STRIDE
        a4 = kvc_raw.reshape(4, batch, seq, HEAD_DIM)
        pes = jnp.stack([ck_pe[l], cv_pe[l]]).reshape(2, 1, CMP_BLOCK * HEAD_DIM)
        w1s = jnp.stack([ck_w1[l], cv_w1[l]]).astype(BF16)
        zero2 = jnp.zeros_like(ck_w2[l])
        w2s = jnp.stack([jnp.concatenate([ck_w2[l], zero2], axis=1),
                         jnp.concatenate([cv_w2[l], cv_w2[l]], axis=1)]).astype(BF16)
        kvc = _compress(a4, pes, w1s, w2s)

        ovt = _overlap_t(n_rows, seq // SEL_BLOCK)
        attn_out = _attention(q_pad, kvc, k4, v4, gates, ovt, batch, seq)

        wo = w_out[l].astype(BF16)
        x2 = _out_ffn(x2, conv_out, attn_out, wo[:D_CONV], wo[D_CONV:],
                      norm2_g[l][None, :], w_ff1[l].astype(BF16), w_ff2[l].astype(BF16),
                      norm_f_g[None, :])
    return x2.reshape(batch, seq, D_MODEL)
```

```python
import functools

import jax
import jax.numpy as jnp
from jax import lax
from jax.experimental import pallas as pl
from jax.experimental.pallas import tpu as pltpu

D_MODEL = 1024
D_CONV = 512
CONV_WIDTH = 31
N_HEADS = 8
HEAD_DIM = 64
N_KV = 2
HPG = N_HEADS // N_KV
D_ATTN = N_HEADS * HEAD_DIM
N_BRANCH = 3
CMP_BLOCK = 32
CMP_STRIDE = 16
CMP_HIDDEN = 256
SEL_BLOCK = 64
SEL_SHIFT = 6
SEL_TOP = 16
WINDOW = 512
D_FF = 4 * D_MODEL
EPS = 1e-6
NEG = -1e30
FORCE = 1e30
MASKED = -1e32

LANES = 128
SUBLANES = 8
F32 = jnp.float32
BF16 = jnp.bfloat16

TM_PROJ = 512
CONV_HALO = 32
CONV_ROWS = 32
TM_FFN = 512
SEL_TILE = 512
Q_TILE = 256
ROW_CHUNK = 32
RANK_WAYS = 2
F_WIN_BLK = HEAD_DIM
F_WIN_OFF = HEAD_DIM + 1
F_WIN_ONE = HEAD_DIM + 2
F_CMP_IDX = HEAD_DIM + 3
WIN_KEYS = WINDOW + Q_TILE
VMEM_LIMIT = 56 * 1024 * 1024

C_U = 0
C_Q = 2 * D_CONV
C_KVC = C_Q + D_ATTN
C_KV = C_KVC + 4 * HEAD_DIM
C_GATE = C_KV + 8 * HEAD_DIM
C_END = C_GATE + LANES

NT_DIMS = (((1,), (1,)), ((), ()))


def _dot(a, b):
    return jnp.dot(a, b, preferred_element_type=F32)


def _dot_nt(a, b):
    return lax.dot_general(a, b, NT_DIMS, preferred_element_type=F32)


def _proj_in_kernel(x_ref, g_ref, w_ref, u_ref, q_ref, kvc_ref, k_ref, v_ref, gate_ref, *, seq):
    i = pl.program_id(0)
    x = x_ref[...]
    ms = jnp.mean(x * x, axis=-1, keepdims=True)
    h = (x * lax.rsqrt(ms + EPS) * g_ref[...]).astype(BF16)

    zu = _dot(h, w_ref[:, C_U:C_Q])
    u_ref[...] = zu[:, :D_CONV] * jax.nn.sigmoid(zu[:, D_CONV:])

    tm = x.shape[0]
    pad64 = jnp.zeros((tm, HEAD_DIM), F32)

    def emit_q():
        zq = _dot(h, w_ref[:, C_Q:C_KVC])
        for hd in range(N_HEADS):
            q_ref[:, hd * LANES:(hd + 1) * LANES] = jnp.concatenate(
                [zq[:, hd * HEAD_DIM:(hd + 1) * HEAD_DIM], pad64], axis=1).astype(BF16)

    def emit_kvc():
        zc = _dot(h, w_ref[:, C_KVC:C_KV])
        for n in range(4):
            kvc_ref[n] = zc[:, n * HEAD_DIM:(n + 1) * HEAD_DIM]

    def emit_kv():
        t = (i * tm) % seq + lax.broadcasted_iota(jnp.int32, (tm, LANES), 0)
        lane = lax.broadcasted_iota(jnp.int32, (tm, LANES), 1)
        feat = jnp.where(lane == LANES - 1, (t & (SEL_BLOCK - 1)).astype(F32),
                         jnp.where(lane - HEAD_DIM == (t >> SEL_SHIFT), 1.0, 0.0))
        win_feat = jnp.where(lane == F_WIN_BLK, (t >> SEL_SHIFT).astype(F32),
                             jnp.where(lane == F_WIN_OFF, (t & (SEL_BLOCK - 1)).astype(F32),
                                       jnp.where(lane == F_WIN_ONE, 1.0, 0.0)))
        ones64 = jnp.ones((tm, HEAD_DIM), F32)
        zkv = _dot(h, w_ref[:, C_KV:C_GATE])
        for n in range(8):
            z = zkv[:, n * HEAD_DIM:(n + 1) * HEAD_DIM]
            kind, grp = divmod(n, N_KV)
            if kind % 2 == 0:
                k = jnp.concatenate([z, pad64], axis=1) + (feat if kind == 0 else win_feat)
                k_ref[kind + grp] = k.astype(BF16)
            else:
                v_ref[kind - 1 + grp] = jnp.concatenate([z, ones64, ones64, z], axis=1).astype(BF16)

    emit_q()
    emit_kvc()
    emit_kv()
    gate_ref[...] = jax.nn.sigmoid(_dot(h, w_ref[:, C_GATE:C_END]))


def _proj_in(x2, g1, w_a, seq):
    n_tok = x2.shape[0]
    grid = (n_tok // TM_PROJ,)
    return pl.pallas_call(
        functools.partial(_proj_in_kernel, seq=seq),
        grid=grid,
        in_specs=[
            pl.BlockSpec((TM_PROJ, D_MODEL), lambda i: (i, 0)),
            pl.BlockSpec((1, D_MODEL), lambda i: (0, 0)),
            pl.BlockSpec((D_MODEL, C_END), lambda i: (0, 0), pipeline_mode=pl.Buffered(1)),
        ],
        out_specs=[
            pl.BlockSpec((TM_PROJ, D_CONV), lambda i: (i, 0)),
            pl.BlockSpec((TM_PROJ, N_HEADS * LANES), lambda i: (i, 0)),
            pl.BlockSpec((4, TM_PROJ, HEAD_DIM), lambda i: (0, i, 0)),
            pl.BlockSpec((4, TM_PROJ, LANES), lambda i: (0, i, 0)),
            pl.BlockSpec((4, TM_PROJ, 2 * LANES), lambda i: (0, i, 0)),
            pl.BlockSpec((TM_PROJ, LANES), lambda i: (i, 0)),
        ],
        out_shape=[
            jax.ShapeDtypeStruct((n_tok, D_CONV), F32),
            jax.ShapeDtypeStruct((n_tok, N_HEADS * LANES), BF16),
            jax.ShapeDtypeStruct((4, n_tok, HEAD_DIM), F32),
            jax.ShapeDtypeStruct((4, n_tok, LANES), BF16),
            jax.ShapeDtypeStruct((4, n_tok, 2 * LANES), BF16),
            jax.ShapeDtypeStruct((n_tok, LANES), F32),
        ],
        compiler_params=pltpu.CompilerParams(
            dimension_semantics=("arbitrary",), vmem_limit_bytes=VMEM_LIMIT),
        name="proj_in",
    )(x2, g1, w_a)


def _compress_kernel(a_ref, pe_ref, w1_ref, w2_ref, o_ref):
    n_rows = a_ref.shape[2] // CMP_STRIDE
    a = jnp.concatenate(
        [a_ref[0, 0, pl.ds(i, n_rows, stride=CMP_STRIDE), :] for i in range(CMP_STRIDE)],
        axis=1).astype(BF16)
    half = CMP_STRIDE * HEAD_DIM
    top = _dot(a, w1_ref[0, :half, :])
    bot = _dot(a, w1_ref[0, half:, :])
    bot = pltpu.roll(bot, n_rows - 1, 0)
    pe = _dot(pe_ref[0].astype(BF16), w1_ref[0])
    hid = jax.nn.gelu(top + bot + pe, approximate=True)
    out = _dot(hid.astype(BF16), w2_ref[0])
    row = lax.broadcasted_iota(jnp.int32, out.shape, 0)
    lane = lax.broadcasted_iota(jnp.int32, out.shape, 1)
    is_key = pl.program_id(0) < N_KV
    out = out + jnp.where((lane == F_CMP_IDX) & is_key, row.astype(F32), 0.0)
    o_ref[0, 0] = jnp.where(row == n_rows - 1, 0.0, out).astype(BF16)


def _compress(a4, pes, w1s, w2s):
    _, batch, seq, _ = a4.shape
    n_rows = seq // CMP_STRIDE
    return pl.pallas_call(
        _compress_kernel,
        grid=(4, batch),
        in_specs=[
            pl.BlockSpec((1, 1, seq, HEAD_DIM), lambda n, b: (n, b, 0, 0)),
            pl.BlockSpec((1, 1, CMP_BLOCK * HEAD_DIM), lambda n, b: (n // N_KV, 0, 0)),
            pl.BlockSpec((1, CMP_BLOCK * HEAD_DIM, CMP_HIDDEN), lambda n, b: (n // N_KV, 0, 0)),
            pl.BlockSpec((1, CMP_HIDDEN, LANES), lambda n, b: (n // N_KV, 0, 0)),
        ],
        out_specs=pl.BlockSpec((1, 1, n_rows, LANES), lambda n, b: (n, b, 0, 0)),
        out_shape=jax.ShapeDtypeStruct((4, batch, n_rows, LANES), BF16),
        compiler_params=pltpu.CompilerParams(
            dimension_semantics=("arbitrary", "arbitrary"), vmem_limit_bytes=VMEM_LIMIT),
        name="compress",
    )(a4, pes, w1s, w2s)


def _conv_tile(u, first, w_ref, b_ref, g_ref, beta_ref, o_ref, ext_ref, sh_ref):
    tm = u.shape[0]
    ext_ref[0:CONV_HALO, :] = jnp.where(first, 0.0, ext_ref[tm:tm + CONV_HALO, :])
    ext_ref[CONV_HALO:, :] = u
    off = CONV_HALO - (CONV_WIDTH - 1)
    n_sh = sh_ref.shape[1]
    for b in range(1, SUBLANES):
        sh_ref[b - 1] = ext_ref[b:b + n_sh, :]
    for r in range(tm // CONV_ROWS):
        r0 = r * CONV_ROWS
        acc = jnp.zeros((CONV_ROWS, D_CONV), F32) + b_ref[...]
        for k in range(CONV_WIDTH):
            a, b = divmod(off + k, SUBLANES)
            lo = r0 + a * SUBLANES
            tap = ext_ref[lo:lo + CONV_ROWS, :] if b == 0 else sh_ref[b - 1, lo:lo + CONV_ROWS, :]
            w_k = w_ref[k * SUBLANES:(k + 1) * SUBLANES, :]
            acc = acc + tap * jnp.concatenate([w_k] * (CONV_ROWS // SUBLANES), axis=0)
        mu = jnp.mean(acc, axis=-1, keepdims=True)
        d = acc - mu
        var = jnp.mean(d * d, axis=-1, keepdims=True)
        y = d * lax.rsqrt(var + EPS) * g_ref[...] + beta_ref[...]
        o_ref[r0:r0 + CONV_ROWS, :] = (y * jax.nn.sigmoid(y)).astype(BF16)


def _exp_rows(s):
    e = jnp.exp(s - jnp.max(s, axis=-1, keepdims=True))
    return e, 1.0 / jnp.sum(e, axis=-1, keepdims=True)


def _attn_kernel(q_ref, kc_ref, vc_ref, ksel_ref, vsel_ref, kwin_ref, vwin_ref,
                 gate_ref, ovt_ref, gexp_ref, o_ref,
                 qa_ref, m_ref, acc_ref, s0_ref, s1_ref, p0_ref, p1_ref,
                 sw_ref, ew_ref, sc_ref, ec_ref, tiles_ref):
    g = pl.program_id(1)
    step = pl.program_id(2)
    q0 = step * Q_TILE
    ci0 = step * (Q_TILE // SEL_BLOCK)
    rows = HPG * Q_TILE
    qb = q_ref[...]
    q_heads = [qb[:, h * LANES:(h + 1) * LANES] for h in range(HPG)]
    slope0 = jnp.where(g == 0, 0.5, 0.5 ** (HPG + 1)).astype(F32)
    slopes = [slope0 * (0.5 ** h) for h in range(HPG)]
    head_rows = lambda a, h: a[h * Q_TILE:(h + 1) * Q_TILE]

    lane = lax.broadcasted_iota(jnp.int32, (Q_TILE, LANES), 1)
    n_rc = Q_TILE // ROW_CHUNK


    win_blk0 = jnp.maximum(ci0 - WINDOW // SEL_BLOCK, 0)
    start = pl.multiple_of(win_blk0 * SEL_BLOCK, SEL_BLOCK)
    q_cw = []
    for h in range(HPG):
        feat = jnp.where(
            lane == F_WIN_BLK, SEL_BLOCK * slopes[h],
            jnp.where(lane == F_WIN_OFF, slopes[h],
                      jnp.where(lane == F_WIN_ONE, -SEL_BLOCK * slopes[h] * win_blk0.astype(F32),
                                jnp.where(lane == F_CMP_IDX, CMP_STRIDE * slopes[h], 0.0))))
        q_cw.append(jnp.where(lane < HEAD_DIM, q_heads[h].astype(F32), feat).astype(BF16))
    q_cw = jnp.concatenate(q_cw, axis=0)

    n_cmp = kc_ref.shape[2]
    sc_ref[...] = _dot_nt(q_cw, kc_ref[0, 0])
    r_cmp = [[None] * n_rc for _ in range(HPG)]
    p_sum = []
    for c in range(n_rc):
        rq = c * ROW_CHUNK + lax.broadcasted_iota(jnp.int32, (ROW_CHUNK, n_cmp), 0)
        cc = lax.broadcasted_iota(jnp.int32, (ROW_CHUNK, n_cmp), 1)
        valid = q0 + rq - cc * CMP_STRIDE >= CMP_BLOCK - 1
        p_chunk = jnp.zeros((ROW_CHUNK, n_cmp), F32)
        for h in range(HPG):
            r0 = h * Q_TILE + c * ROW_CHUNK
            e, r_sum = _exp_rows(jnp.where(valid, sc_ref[r0:r0 + ROW_CHUNK, :], NEG))
            p_chunk = p_chunk + e * r_sum
            ec_ref[r0:r0 + ROW_CHUNK, :] = e.astype(BF16)
            r_cmp[h][c] = r_sum
        p_sum.append(p_chunk)
    p_sum = jnp.concatenate(p_sum, axis=0)
    o_cmp_all = _dot(ec_ref[...], vc_ref[0, 0])

    p_hi = p_sum.astype(BF16)
    p_lo = (p_sum - p_hi.astype(F32)).astype(BF16)
    ovt = ovt_ref[...]
    imp = _dot_nt(ovt, p_hi) + _dot_nt(ovt, p_lo)
    n_sel = imp.shape[0]
    jrow = lax.broadcasted_iota(jnp.int32, (n_sel, Q_TILE), 0)
    qlane = lax.broadcasted_iota(jnp.int32, (n_sel, Q_TILE), 1)
    cl = ci0 + (qlane >> SEL_SHIFT)
    forced = (jrow == 0) | (jrow == cl) | (jrow == cl - 1)
    imp = jnp.where(forced, FORCE, jnp.where(jrow > cl, NEG, imp))
    sw_ref[...] = _dot_nt(q_cw, kwin_ref[0, pl.ds(start, WIN_KEYS), :])
    groups = [imp[8 * v:8 * v + 8, :] for v in range(n_sel // 8)]
    sub = lax.broadcasted_iota(jnp.int32, (8, Q_TILE), 0)
    ranks = [[jnp.zeros((8, Q_TILE), jnp.int32) for _ in range(RANK_WAYS)] for _ in groups]
    for jp in range(n_sel):
        row = groups[jp // 8][jp % 8:jp % 8 + 1, :]
        w = jp % RANK_WAYS
        for v in range(n_sel // 8):
            x = groups[v]
            if 8 * v + 7 < jp:
                ranks[v][w] = ranks[v][w] + jnp.where(row > x, 1, 0)
            elif 8 * v > jp:
                ranks[v][w] = ranks[v][w] + jnp.where(row >= x, 1, 0)
            else:
                tie = jnp.where(sub + 8 * v > jp, 1, 0)
                ranks[v][w] = (ranks[v][w] + jnp.where(row > x, 1, 0)
                               + jnp.where(row == x, tie, 0))
    rank = jnp.concatenate([sum(r[1:], r[0]) for r in ranks], axis=0)
    sel_t = jnp.where((rank < SEL_TOP) & (jrow <= cl), 1.0, 0.0)

    tile_blocks = SEL_TILE // SEL_BLOCK
    n_full = (ci0 + Q_TILE // SEL_BLOCK - 1) // tile_blocks
    n_active = jnp.int32(0)
    for t in range(n_sel // tile_blocks):
        wanted = jnp.max(sel_t[t * tile_blocks:(t + 1) * tile_blocks, :]) > 0.5
        tiles_ref[n_active] = t
        n_active = n_active + jnp.where(wanted & (t < n_full), 1, 0)
    tiles_ref[n_active] = n_full

    sel_t = jnp.concatenate([jnp.zeros((LANES - n_sel, Q_TILE), F32), sel_t], axis=0)
    sel_q = jnp.concatenate(
        [sel_t[:, c * LANES:(c + 1) * LANES].T for c in range(Q_TILE // LANES)], axis=0)
    qrow = lax.broadcasted_iota(jnp.int32, (Q_TILE, LANES), 0)
    base = ((lane - HEAD_DIM - ci0 - (qrow >> SEL_SHIFT)) * SEL_BLOCK).astype(F32)
    q_aug = []
    for h in range(HPG):
        feat = jnp.where(lane == LANES - 1, slopes[h],
                         jnp.where(sel_q > 0.5, slopes[h] * base, MASKED))
        q_aug.append(jnp.where(lane < HEAD_DIM, q_heads[h].astype(F32), feat).astype(BF16))
    q_aug = jnp.concatenate(q_aug, axis=0)

    def k_tile(kt):
        return ksel_ref[0, pl.ds(pl.multiple_of(kt * SEL_TILE, SEL_TILE), SEL_TILE), :]

    def v_tile(kt):
        return vsel_ref[0, pl.ds(pl.multiple_of(kt * SEL_TILE, SEL_TILE), SEL_TILE), :]

    s_bufs = (s0_ref, s1_ref)
    p_bufs = (p0_ref, p1_ref)
    qa_ref[...] = q_aug
    m_ref[...] = jnp.full((rows, LANES), NEG, F32)
    acc_ref[...] = jnp.zeros((rows, 2 * LANES), F32)
    p1_ref[...] = jnp.zeros((rows, SEL_TILE), BF16)
    s0_ref[...] = _dot_nt(q_aug, k_tile(tiles_ref[0]))

    o_win_all = []
    for pair in range(HPG // 2):
        for c in range(n_rc):
            rq = c * ROW_CHUNK + lax.broadcasted_iota(jnp.int32, (ROW_CHUNK, WIN_KEYS), 0)
            dw = q0 + rq - start - lax.broadcasted_iota(jnp.int32, (ROW_CHUNK, WIN_KEYS), 1)
            valid_w = (dw | (WINDOW - 1 - dw)) >= 0
            for h in (2 * pair, 2 * pair + 1):
                r0 = h * Q_TILE + c * ROW_CHUNK
                s = jnp.where(valid_w, sw_ref[r0:r0 + ROW_CHUNK, :], NEG)
                e = jnp.exp(s - jnp.max(s, axis=-1, keepdims=True))
                ew_ref[r0:r0 + ROW_CHUNK, :] = e.astype(BF16)
        pair_rows = slice(2 * pair * Q_TILE, (2 * pair + 2) * Q_TILE)
        o_win_all.append(_dot(ew_ref[pair_rows, :], vwin_ref[0, pl.ds(start, WIN_KEYS), :]))
    o_win_all = jnp.concatenate(o_win_all, axis=0)

    def online(s_ref, p_ref, pv, causal_tile, width=SEL_TILE):
        for c in range(n_rc):
            if causal_tile:
                kpos = n_full * SEL_TILE + lax.broadcasted_iota(jnp.int32, (ROW_CHUNK, width), 1)
                tq = q0 + c * ROW_CHUNK + lax.broadcasted_iota(jnp.int32, (ROW_CHUNK, width), 0)
                causal = kpos <= tq
            for h in range(HPG):
                r0 = h * Q_TILE + c * ROW_CHUNK
                rs = slice(r0, r0 + ROW_CHUNK)
                s = s_ref[rs, :width]
                if causal_tile:
                    s = jnp.where(causal, s, MASKED)
                m_old = m_ref[rs, :]
                m_new = jnp.maximum(m_old, jnp.max(s, axis=-1, keepdims=True))
                m_ref[rs, :] = m_new
                p = [jnp.exp(s[:, j * LANES:(j + 1) * LANES] - m_new)
                     for j in range(width // LANES)]
                p_ref[rs, :width] = jnp.concatenate(p, axis=1).astype(BF16)
                alpha = jnp.exp(m_old - m_new)
                acc_ref[rs, :] = jnp.concatenate([alpha, alpha], axis=1) * (acc_ref[rs, :]
                                                                           + pv[rs, :])

    def stage(i, par):
        s_bufs[1 - par][...] = _dot_nt(qa_ref[...], k_tile(tiles_ref[i + 1]))
        pv = _dot(p_bufs[1 - par][...], v_tile(tiles_ref[jnp.maximum(i - 1, 0)]))
        online(s_bufs[par], p_bufs[par], pv, False)

    def sweep(i, carry):
        lax.cond(i % 2 == 0, lambda: stage(i, 0), lambda: stage(i, 1))
        return carry

    lax.fori_loop(0, n_active, sweep, 0)

    def last(par, width):
        pv = _dot(p_bufs[1 - par][...], v_tile(tiles_ref[jnp.maximum(n_active - 1, 0)]))
        online(s_bufs[par], p_bufs[par], pv, True, width)
        own_v = vsel_ref[0, pl.ds(pl.multiple_of(n_full * SEL_TILE, SEL_TILE), width), :]
        acc_ref[...] += _dot(p_bufs[par][:, :width], own_v)

    def finish(width):
        lax.cond(n_active % 2 == 0, lambda: last(0, width), lambda: last(1, width))

    own_last = (ci0 + Q_TILE // SEL_BLOCK - 1) % tile_blocks
    lax.cond(own_last < tile_blocks // 2, lambda: finish(SEL_TILE // 2), lambda: finish(SEL_TILE))
    o_sel_all = acc_ref[...]

    gates = gate_ref[...]
    g_hi = gates.astype(BF16)
    g_lo = (gates - g_hi.astype(F32)).astype(BF16)
    expand = gexp_ref[...]
    g_wide = _dot(g_hi, expand) + _dot(g_lo, expand)
    even_lanes = lane < HEAD_DIM
    sees_cmp = q0 + lax.broadcasted_iota(jnp.int32, (Q_TILE, LANES), 0) >= CMP_BLOCK - 1
    pairs = []
    for pair in range(HPG // 2):
        h0, h1 = 2 * pair, 2 * pair + 1
        def gate(n, pair=pair):
            blk = n * (HPG // 2) + pair
            return g_wide[:, blk * LANES:(blk + 1) * LANES]
        def both(x0, x1):
            return jnp.where(even_lanes, x0, x1)
        o_cmp = both(head_rows(o_cmp_all, h0) * jnp.concatenate(r_cmp[h0], axis=0),
                     head_rows(o_cmp_all, h1) * jnp.concatenate(r_cmp[h1], axis=0))
        o_cmp = jnp.where(sees_cmp, o_cmp, 0.0)
        def normalised(acc_all):
            a0, a1 = head_rows(acc_all, h0), head_rows(acc_all, h1)
            return both(a0[:, :LANES], a1[:, LANES:]) * (1.0 / both(a0[:, LANES:], a1[:, :LANES]))
        pairs.append(gate(0) * o_cmp + gate(1) * normalised(o_sel_all)
                     + gate(2) * normalised(o_win_all))
    o_ref[...] = jnp.concatenate(pairs, axis=1).astype(BF16)


def _gate_expand():
    col = jnp.arange(N_KV * N_BRANCH * (HPG // 2) * LANES)
    blk, h_in_pair = col // LANES, (col % LANES) // HEAD_DIM
    g, n, pair = blk // (N_BRANCH * HPG // 2), (blk // (HPG // 2)) % N_BRANCH, blk % (HPG // 2)
    row = jnp.arange(LANES)[:, None]
    return (row == (n * N_HEADS + g * HPG + 2 * pair + h_in_pair)[None, :]).astype(BF16)


def _attention(q_pad, kvc, k4, v4, gates, ovt, batch, seq):
    n_steps = seq // Q_TILE
    rows = HPG * Q_TILE
    n_cmp = kvc.shape[2]
    n_sel = ovt.shape[0]
    k_spec = lambda branch: pl.BlockSpec(
        (1, seq, LANES), lambda b, g, c, branch=branch: (branch * N_KV + g, b, 0))
    v_spec = lambda branch: pl.BlockSpec(
        (1, seq, 2 * LANES), lambda b, g, c, branch=branch: (branch * N_KV + g, b, 0))
    return pl.pallas_call(
        _attn_kernel,
        grid=(batch, N_KV, n_steps),
        in_specs=[
            pl.BlockSpec((Q_TILE, HPG * LANES), lambda b, g, c: (b * n_steps + c, g)),
            pl.BlockSpec((1, 1, n_cmp, LANES), lambda b, g, c: (g, b, 0, 0)),
            pl.BlockSpec((1, 1, n_cmp, LANES), lambda b, g, c: (N_KV + g, b, 0, 0)),
            k_spec(0), v_spec(0), k_spec(1), v_spec(1),
            pl.BlockSpec((Q_TILE, LANES), lambda b, g, c: (b * n_steps + c, 0)),
            pl.BlockSpec((n_sel, n_cmp), lambda b, g, c: (0, 0)),
            pl.BlockSpec((LANES, N_BRANCH * (HPG // 2) * LANES), lambda b, g, c: (0, g)),
        ],
        out_specs=pl.BlockSpec((Q_TILE, HPG * HEAD_DIM), lambda b, g, c: (b * n_steps + c, g)),
        out_shape=jax.ShapeDtypeStruct((batch * seq, D_ATTN), BF16),
        scratch_shapes=[
            pltpu.VMEM((rows, LANES), BF16),
            pltpu.VMEM((rows, LANES), F32),
            pltpu.VMEM((rows, 2 * LANES), F32),
            pltpu.VMEM((rows, SEL_TILE), F32), pltpu.VMEM((rows, SEL_TILE), F32),
            pltpu.VMEM((rows, SEL_TILE), BF16), pltpu.VMEM((rows, SEL_TILE), BF16),
            pltpu.VMEM((rows, WIN_KEYS), F32), pltpu.VMEM((rows, WIN_KEYS), BF16),
            pltpu.VMEM((rows, n_cmp), F32), pltpu.VMEM((rows, n_cmp), BF16),
            pltpu.SMEM((seq // SEL_TILE + 1,), jnp.int32),
        ],
        compiler_params=pltpu.CompilerParams(
            dimension_semantics=("arbitrary", "arbitrary", "arbitrary"),
            vmem_limit_bytes=VMEM_LIMIT),
        name="attn",
    )(q_pad, kvc, kvc, k4, v4, k4, v4, gates, ovt, _gate_expand())


def _rms(x, g):
    return x * lax.rsqrt(jnp.mean(x * x, axis=-1, keepdims=True) + EPS) * g


def _out_ffn_kernel(x_ref, u0_ref, un_ref, a_ref, cw_ref, cb_ref, cg_ref, cbeta_ref,
                    wc_ref, wa_ref, g2_ref, w1_ref, w2_ref, gf_ref, o_ref,
                    conv_ref, ext_ref, sh_ref, *, seq):
    i = pl.program_id(0)
    tm = x_ref.shape[0]
    conv_args = (cw_ref, cb_ref, cg_ref, cbeta_ref, conv_ref, ext_ref, sh_ref)

    @pl.when(i == 0)
    def _():
        ext_ref[...] = jnp.zeros(ext_ref.shape, F32)
        _conv_tile(u0_ref[...], True, *conv_args)

    conv_cur = conv_ref[...]
    _conv_tile(un_ref[...], ((i + 1) * tm) % seq == 0, *conv_args)
    x1 = x_ref[...] + _dot(conv_cur, wc_ref[...]) + _dot(a_ref[...], wa_ref[...])
    h2 = _rms(x1, g2_ref[...]).astype(BF16)
    a = jnp.maximum(_dot(h2, w1_ref[...]), 0.0)
    y = x1 + _dot((a * a).astype(BF16), w2_ref[...])
    o_ref[...] = _rms(y, gf_ref[...])


def _out_ffn(x2, u, attn_out, conv_w, conv_b, cln_g, cln_b, wc, wa, g2, w1, w2, gf, seq):
    n_tok = x2.shape[0]
    n_tiles = n_tok // TM_FFN
    const = lambda shape: pl.BlockSpec(shape, lambda i: (0, 0), pipeline_mode=pl.Buffered(1))
    return pl.pallas_call(
        functools.partial(_out_ffn_kernel, seq=seq),
        grid=(n_tiles,),
        in_specs=[
            pl.BlockSpec((TM_FFN, D_MODEL), lambda i: (i, 0)),
            const((TM_FFN, D_CONV)),
            pl.BlockSpec((TM_FFN, D_CONV), lambda i: (jnp.minimum(i + 1, n_tiles - 1), 0)),
            pl.BlockSpec((TM_FFN, D_ATTN), lambda i: (i, 0)),
            const((CONV_WIDTH * SUBLANES, D_CONV)),
            const((1, D_CONV)), const((1, D_CONV)), const((1, D_CONV)),
            const((D_CONV, D_MODEL)),
            const((D_ATTN, D_MODEL)),
            const((1, D_MODEL)),
            const((D_MODEL, D_FF)),
            const((D_FF, D_MODEL)),
            const((1, D_MODEL)),
        ],
        out_specs=pl.BlockSpec((TM_FFN, D_MODEL), lambda i: (i, 0)),
        out_shape=jax.ShapeDtypeStruct((n_tok, D_MODEL), F32),
        scratch_shapes=[
            pltpu.VMEM((TM_FFN, D_CONV), BF16),
            pltpu.VMEM((TM_FFN + CONV_HALO, D_CONV), F32),
            pltpu.VMEM((SUBLANES - 1, TM_FFN + CONV_HALO - SUBLANES, D_CONV), F32),
        ],
        compiler_params=pltpu.CompilerParams(
            dimension_semantics=("arbitrary",), vmem_limit_bytes=VMEM_LIMIT),
        name="out_ffn",
    )(x2, u, u, attn_out, conv_w, conv_b, cln_g, cln_b, wc, wa, g2, w1, w2, gf)


def _pack_w_in(w_in):
    o2 = 2 * D_CONV
    o3 = o2 + D_ATTN
    o4 = o3 + 2 * N_BRANCH * N_KV * HEAD_DIM
    gate_w = w_in[:, o4:].reshape(D_MODEL, N_HEADS, N_BRANCH).transpose(0, 2, 1)
    gate_w = gate_w.reshape(D_MODEL, N_BRANCH * N_HEADS)
    cols = [w_in[:, :o2], w_in[:, o2:o3] * HEAD_DIM ** -0.5, w_in[:, o3:o4], gate_w,
            jnp.zeros((D_MODEL, LANES - N_BRANCH * N_HEADS), w_in.dtype)]
    return jnp.concatenate(cols, axis=1).astype(BF16)


def _overlap_t(n_cmp_rows, n_sel):
    c = jnp.arange(n_cmp_rows)[None, :]
    j = jnp.arange(n_sel)[:, None]
    ov = (c * CMP_STRIDE < (j + 1) * SEL_BLOCK) & (c * CMP_STRIDE + CMP_BLOCK > j * SEL_BLOCK)
    return ov.astype(BF16)


def kernel(x, norm1_g, w_in, dw_w, dw_b, cln_g, cln_b, ck_pe, ck_w1, ck_w2, cv_pe, cv_w1, cv_w2,
           w_out, norm2_g, w_ff1, w_ff2, norm_f_g):
    batch, seq, _ = x.shape
    n_tok = batch * seq
    x2 = x.reshape(n_tok, D_MODEL)
    assert norm1_g.shape[0] == 1, "the final norm is fused into the (single) layer's last kernel"
    for l in range(1):
        w_a = _pack_w_in(w_in[l])
        u, q_pad, kvc_raw, k4, v4, gates = _proj_in(x2, norm1_g[l][None, :], w_a, seq)

        n_rows = seq // CMP_STRIDE
        a4 = kvc_raw.reshape(4, batch, seq, HEAD_DIM)
        pes = jnp.stack([ck_pe[l], cv_pe[l]]).reshape(2, 1, CMP_BLOCK * HEAD_DIM)
        w1s = jnp.stack([ck_w1[l], cv_w1[l]]).astype(BF16)
        zero2 = jnp.zeros_like(ck_w2[l])
        w2s = jnp.stack([jnp.concatenate([ck_w2[l], zero2], axis=1),
                         jnp.concatenate([cv_w2[l], cv_w2[l]], axis=1)]).astype(BF16)
        kvc = _compress(a4, pes, w1s, w2s)

        ovt = _overlap_t(n_rows, seq // SEL_BLOCK)
        attn_out = _attention(q_pad, kvc, k4, v4, gates, ovt, batch, seq)

        wo = w_out[l].astype(BF16)
        conv_w = jnp.repeat(dw_w[l][:, 0, :], SUBLANES, axis=0)
        x2 = _out_ffn(x2, u, attn_out, conv_w, dw_b[l][None, :], cln_g[l][None, :],
                      cln_b[l][None, :], wo[:D_CONV], wo[D_CONV:], norm2_g[l][None, :],
                      w_ff1[l].astype(BF16), w_ff2[l].astype(BF16), norm_f_g[None, :], seq)
    return x2.reshape(batch, seq, D_MODEL)
```

```python
import functools

import jax
import jax.numpy as jnp
from jax import lax
from jax.experimental import pallas as pl
from jax.experimental.pallas import tpu as pltpu

D_MODEL = 1024
D_CONV = 512
CONV_WIDTH = 31
N_HEADS = 8
HEAD_DIM = 64
N_KV = 2
HPG = N_HEADS // N_KV
D_ATTN = N_HEADS * HEAD_DIM
N_BRANCH = 3
CMP_BLOCK = 32
CMP_STRIDE = 16
CMP_HIDDEN = 256
SEL_BLOCK = 64
SEL_SHIFT = 6
SEL_TOP = 16
WINDOW = 512
D_FF = 4 * D_MODEL
EPS = 1e-6
NEG = -1e30
FORCE = 1e30
MASKED = -1e32

LANES = 128
SUBLANES = 8
F32 = jnp.float32
BF16 = jnp.bfloat16

TM_PROJ = 512
CONV_HALO = 32
CONV_ROWS = 16
TM_FFN = 512
SEL_TILE = 512
Q_TILE = 256
ROW_CHUNK = 64
RANK_WAYS = 2
F_WIN_BLK = HEAD_DIM
F_WIN_OFF = HEAD_DIM + 1
F_WIN_ONE = HEAD_DIM + 2
F_CMP_IDX = HEAD_DIM + 3
WIN_KEYS = WINDOW + Q_TILE
VMEM_LIMIT = 56 * 1024 * 1024

C_U = 0
C_Q = 2 * D_CONV
C_KVC = C_Q + D_ATTN
C_KV = C_KVC + 4 * HEAD_DIM
C_GATE = C_KV + 8 * HEAD_DIM
C_END = C_GATE + LANES

NT_DIMS = (((1,), (1,)), ((), ()))


def _dot(a, b):
    return jnp.dot(a, b, preferred_element_type=F32)


def _dot_nt(a, b):
    return lax.dot_general(a, b, NT_DIMS, preferred_element_type=F32)


def _proj_in_kernel(x_ref, g_ref, w_ref, u_ref, q_ref, kvc_ref, k_ref, v_ref, gate_ref, *, seq):
    i = pl.program_id(0)
    x = x_ref[...]
    ms = jnp.mean(x * x, axis=-1, keepdims=True)
    h = (x * lax.rsqrt(ms + EPS) * g_ref[...]).astype(BF16)

    zu = _dot(h, w_ref[:, C_U:C_Q])
    u_ref[...] = zu[:, :D_CONV] * jax.nn.sigmoid(zu[:, D_CONV:])

    tm = x.shape[0]
    pad64 = jnp.zeros((tm, HEAD_DIM), F32)

    def emit_q():
        zq = _dot(h, w_ref[:, C_Q:C_KVC])
        for hd in range(N_HEADS):
            q_ref[:, hd * LANES:(hd + 1) * LANES] = jnp.concatenate(
                [zq[:, hd * HEAD_DIM:(hd + 1) * HEAD_DIM], pad64], axis=1).astype(BF16)

    def emit_kvc():
        zc = _dot(h, w_ref[:, C_KVC:C_KV])
        for n in range(4):
            kvc_ref[n] = zc[:, n * HEAD_DIM:(n + 1) * HEAD_DIM]

    def emit_kv():
        t = (i * tm) % seq + lax.broadcasted_iota(jnp.int32, (tm, LANES), 0)
        lane = lax.broadcasted_iota(jnp.int32, (tm, LANES), 1)
        feat = jnp.where(lane == LANES - 1, (t & (SEL_BLOCK - 1)).astype(F32),
                         jnp.where(lane - HEAD_DIM == (t >> SEL_SHIFT), 1.0, 0.0))
        win_feat = jnp.where(lane == F_WIN_BLK, (t >> SEL_SHIFT).astype(F32),
                             jnp.where(lane == F_WIN_OFF, (t & (SEL_BLOCK - 1)).astype(F32),
                                       jnp.where(lane == F_WIN_ONE, 1.0, 0.0)))
        ones64 = jnp.ones((tm, HEAD_DIM), F32)
        zkv = _dot(h, w_ref[:, C_KV:C_GATE])
        for n in range(8):
            z = zkv[:, n * HEAD_DIM:(n + 1) * HEAD_DIM]
            kind, grp = divmod(n, N_KV)
            if kind % 2 == 0:
                k = jnp.concatenate([z, pad64], axis=1) + (feat if kind == 0 else win_feat)
                k_ref[kind + grp] = k.astype(BF16)
            else:
                v_ref[kind - 1 + grp] = jnp.concatenate([z, ones64, ones64, z], axis=1).astype(BF16)

    emit_q()
    emit_kvc()
    emit_kv()
    gate_ref[...] = jax.nn.sigmoid(_dot(h, w_ref[:, C_GATE:C_END]))


def _proj_in(x2, g1, w_a, seq):
    n_tok = x2.shape[0]
    grid = (n_tok // TM_PROJ,)
    return pl.pallas_call(
        functools.partial(_proj_in_kernel, seq=seq),
        grid=grid,
        in_specs=[
            pl.BlockSpec((TM_PROJ, D_MODEL), lambda i: (i, 0)),
            pl.BlockSpec((1, D_MODEL), lambda i: (0, 0)),
            pl.BlockSpec((D_MODEL, C_END), lambda i: (0, 0), pipeline_mode=pl.Buffered(1)),
        ],
        out_specs=[
            pl.BlockSpec((TM_PROJ, D_CONV), lambda i: (i, 0)),
            pl.BlockSpec((TM_PROJ, N_HEADS * LANES), lambda i: (i, 0)),
            pl.BlockSpec((4, TM_PROJ, HEAD_DIM), lambda i: (0, i, 0)),
            pl.BlockSpec((4, TM_PROJ, LANES), lambda i: (0, i, 0)),
            pl.BlockSpec((4, TM_PROJ, 2 * LANES), lambda i: (0, i, 0)),
            pl.BlockSpec((TM_PROJ, LANES), lambda i: (i, 0)),
        ],
        out_shape=[
            jax.ShapeDtypeStruct((n_tok, D_CONV), F32),
            jax.ShapeDtypeStruct((n_tok, N_HEADS * LANES), BF16),
            jax.ShapeDtypeStruct((4, n_tok, HEAD_DIM), F32),
            jax.ShapeDtypeStruct((4, n_tok, LANES), BF16),
            jax.ShapeDtypeStruct((4, n_tok, 2 * LANES), BF16),
            jax.ShapeDtypeStruct((n_tok, LANES), F32),
        ],
        compiler_params=pltpu.CompilerParams(
            dimension_semantics=("arbitrary",), vmem_limit_bytes=VMEM_LIMIT),
        name="proj_in",
    )(x2, g1, w_a)


def _compress_kernel(a_ref, pe_ref, w1_ref, w2_ref, o_ref):
    n_rows = a_ref.shape[2] // CMP_STRIDE
    a = jnp.concatenate(
        [a_ref[0, 0, pl.ds(i, n_rows, stride=CMP_STRIDE), :] for i in range(CMP_STRIDE)],
        axis=1).astype(BF16)
    half = CMP_STRIDE * HEAD_DIM
    top = _dot(a, w1_ref[0, :half, :])
    bot = _dot(a, w1_ref[0, half:, :])
    bot = pltpu.roll(bot, n_rows - 1, 0)
    pe = _dot(pe_ref[0].astype(BF16), w1_ref[0])
    hid = jax.nn.gelu(top + bot + pe, approximate=True)
    out = _dot(hid.astype(BF16), w2_ref[0])
    row = lax.broadcasted_iota(jnp.int32, out.shape, 0)
    lane = lax.broadcasted_iota(jnp.int32, out.shape, 1)
    is_key = pl.program_id(0) < N_KV
    out = out + jnp.where((lane == F_CMP_IDX) & is_key, row.astype(F32), 0.0)
    o_ref[0, 0] = jnp.where(row == n_rows - 1, 0.0, out).astype(BF16)


def _compress(a4, pes, w1s, w2s):
    _, batch, seq, _ = a4.shape
    n_rows = seq // CMP_STRIDE
    return pl.pallas_call(
        _compress_kernel,
        grid=(4, batch),
        in_specs=[
            pl.BlockSpec((1, 1, seq, HEAD_DIM), lambda n, b: (n, b, 0, 0)),
            pl.BlockSpec((1, 1, CMP_BLOCK * HEAD_DIM), lambda n, b: (n // N_KV, 0, 0)),
            pl.BlockSpec((1, CMP_BLOCK * HEAD_DIM, CMP_HIDDEN), lambda n, b: (n // N_KV, 0, 0)),
            pl.BlockSpec((1, CMP_HIDDEN, LANES), lambda n, b: (n // N_KV, 0, 0)),
        ],
        out_specs=pl.BlockSpec((1, 1, n_rows, LANES), lambda n, b: (n, b, 0, 0)),
        out_shape=jax.ShapeDtypeStruct((4, batch, n_rows, LANES), BF16),
        compiler_params=pltpu.CompilerParams(
            dimension_semantics=("arbitrary", "arbitrary"), vmem_limit_bytes=VMEM_LIMIT),
        name="compress",
    )(a4, pes, w1s, w2s)


def _conv_tile(u, first, w_ref, b_ref, g_ref, beta_ref, o_ref, ext_ref, sh_ref):
    tm = u.shape[0]
    ext_ref[0:CONV_HALO, :] = jnp.where(first, 0.0, ext_ref[tm:tm + CONV_HALO, :])
    ext_ref[CONV_HALO:, :] = u
    off = CONV_HALO - (CONV_WIDTH - 1)
    n_sh = sh_ref.shape[1]
    for b in range(1, SUBLANES):
        sh_ref[b - 1] = ext_ref[b:b + n_sh, :]
    for r in range(tm // CONV_ROWS):
        r0 = r * CONV_ROWS
        acc = jnp.zeros((CONV_ROWS, D_CONV), F32) + b_ref[...]
        for k in range(CONV_WIDTH):
            a, b = divmod(off + k, SUBLANES)
            lo = r0 + a * SUBLANES
            tap = ext_ref[lo:lo + CONV_ROWS, :] if b == 0 else sh_ref[b - 1, lo:lo + CONV_ROWS, :]
            w_k = w_ref[k * SUBLANES:(k + 1) * SUBLANES, :]
            acc = acc + tap * jnp.concatenate([w_k] * (CONV_ROWS // SUBLANES), axis=0)
        mu = jnp.mean(acc, axis=-1, keepdims=True)
        d = acc - mu
        var = jnp.mean(d * d, axis=-1, keepdims=True)
        y = d * lax.rsqrt(var + EPS) * g_ref[...] + beta_ref[...]
        o_ref[r0:r0 + CONV_ROWS, :] = (y * jax.nn.sigmoid(y)).astype(BF16)


def _exp_rows(s):
    e = jnp.exp(s - jnp.max(s, axis=-1, keepdims=True))
    return e, 1.0 / jnp.sum(e, axis=-1, keepdims=True)


def _attn_kernel(q_ref, kc_ref, vc_ref, ksel_ref, vsel_ref, kwin_ref, vwin_ref,
                 gate_ref, ovt_ref, gexp_ref, o_ref,
                 qa_ref, m_ref, acc_ref, s0_ref, s1_ref, p0_ref, p1_ref,
                 sw_ref, ew_ref, sc_ref, ec_ref, tiles_ref):
    g = pl.program_id(1)
    step = pl.program_id(2)
    q0 = step * Q_TILE
    ci0 = step * (Q_TILE // SEL_BLOCK)
    rows = HPG * Q_TILE
    qb = q_ref[...]
    q_heads = [qb[:, h * LANES:(h + 1) * LANES] for h in range(HPG)]
    slope0 = jnp.where(g == 0, 0.5, 0.5 ** (HPG + 1)).astype(F32)
    slopes = [slope0 * (0.5 ** h) for h in range(HPG)]
    head_rows = lambda a, h: a[h * Q_TILE:(h + 1) * Q_TILE]

    lane = lax.broadcasted_iota(jnp.int32, (Q_TILE, LANES), 1)
    n_rc = Q_TILE // ROW_CHUNK


    win_blk0 = jnp.maximum(ci0 - WINDOW // SEL_BLOCK, 0)
    start = pl.multiple_of(win_blk0 * SEL_BLOCK, SEL_BLOCK)
    q_cw = []
    for h in range(HPG):
        feat = jnp.where(
            lane == F_WIN_BLK, SEL_BLOCK * slopes[h],
            jnp.where(lane == F_WIN_OFF, slopes[h],
                      jnp.where(lane == F_WIN_ONE, -SEL_BLOCK * slopes[h] * win_blk0.astype(F32),
                                jnp.where(lane == F_CMP_IDX, CMP_STRIDE * slopes[h], 0.0))))
        q_cw.append(jnp.where(lane < HEAD_DIM, q_heads[h].astype(F32), feat).astype(BF16))
    q_cw = jnp.concatenate(q_cw, axis=0)

    n_cmp = kc_ref.shape[2]
    sc_ref[...] = _dot_nt(q_cw, kc_ref[0, 0])
    r_cmp = [[None] * n_rc for _ in range(HPG)]
    p_sum = []
    for c in range(n_rc):
        rq = c * ROW_CHUNK + lax.broadcasted_iota(jnp.int32, (ROW_CHUNK, n_cmp), 0)
        cc = lax.broadcasted_iota(jnp.int32, (ROW_CHUNK, n_cmp), 1)
        valid = q0 + rq - cc * CMP_STRIDE >= CMP_BLOCK - 1
        p_chunk = jnp.zeros((ROW_CHUNK, n_cmp), F32)
        for h in range(HPG):
            r0 = h * Q_TILE + c * ROW_CHUNK
            e, r_sum = _exp_rows(jnp.where(valid, sc_ref[r0:r0 + ROW_CHUNK, :], NEG))
            p_chunk = p_chunk + e * r_sum
            ec_ref[r0:r0 + ROW_CHUNK, :] = e.astype(BF16)
            r_cmp[h][c] = r_sum
        p_sum.append(p_chunk)
    p_sum = jnp.concatenate(p_sum, axis=0)
    o_cmp_all = _dot(ec_ref[...], vc_ref[0, 0])

    p_hi = p_sum.astype(BF16)
    p_lo = (p_sum - p_hi.astype(F32)).astype(BF16)
    ovt = ovt_ref[...]
    imp = _dot_nt(ovt, p_hi) + _dot_nt(ovt, p_lo)
    n_sel = imp.shape[0]
    jrow = lax.broadcasted_iota(jnp.int32, (n_sel, Q_TILE), 0)
    qlane = lax.broadcasted_iota(jnp.int32, (n_sel, Q_TILE), 1)
    cl = ci0 + (qlane >> SEL_SHIFT)
    forced = (jrow == 0) | (jrow == cl) | (jrow == cl - 1)
    imp = jnp.where(forced, FORCE, jnp.where(jrow > cl, NEG, imp))
    sw_ref[...] = _dot_nt(q_cw, kwin_ref[0, pl.ds(start, WIN_KEYS), :])
    groups = [imp[8 * v:8 * v + 8, :] for v in range(n_sel // 8)]
    sub = lax.broadcasted_iota(jnp.int32, (8, Q_TILE), 0)
    ranks = [[jnp.zeros((8, Q_TILE), jnp.int32) for _ in range(RANK_WAYS)] for _ in groups]
    for jp in range(n_sel):
        row = groups[jp // 8][jp % 8:jp % 8 + 1, :]
        w = jp % RANK_WAYS
        for v in range(n_sel // 8):
            x = groups[v]
            if 8 * v + 7 < jp:
                ranks[v][w] = ranks[v][w] + jnp.where(row > x, 1, 0)
            elif 8 * v > jp:
                ranks[v][w] = ranks[v][w] + jnp.where(row >= x, 1, 0)
            else:
                tie = jnp.where(sub + 8 * v > jp, 1, 0)
                ranks[v][w] = (ranks[v][w] + jnp.where(row > x, 1, 0)
                               + jnp.where(row == x, tie, 0))
    rank = jnp.concatenate([sum(r[1:], r[0]) for r in ranks], axis=0)
    sel_t = jnp.where((rank < SEL_TOP) & (jrow <= cl), 1.0, 0.0)

    tile_blocks = SEL_TILE // SEL_BLOCK
    n_full = (ci0 + Q_TILE // SEL_BLOCK - 1) // tile_blocks
    n_active = jnp.int32(0)
    for t in range(n_sel // tile_blocks):
        wanted = jnp.max(sel_t[t * tile_blocks:(t + 1) * tile_blocks, :]) > 0.5
        tiles_ref[n_active] = t
        n_active = n_active + jnp.where(wanted & (t < n_full), 1, 0)
    tiles_ref[n_active] = n_full

    sel_t = jnp.concatenate([jnp.zeros((LANES - n_sel, Q_TILE), F32), sel_t], axis=0)
    sel_q = jnp.concatenate(
        [sel_t[:, c * LANES:(c + 1) * LANES].T for c in range(Q_TILE // LANES)], axis=0)
    qrow = lax.broadcasted_iota(jnp.int32, (Q_TILE, LANES), 0)
    base = ((lane - HEAD_DIM - ci0 - (qrow >> SEL_SHIFT)) * SEL_BLOCK).astype(F32)
    q_aug = []
    for h in range(HPG):
        feat = jnp.where(lane == LANES - 1, slopes[h],
                         jnp.where(sel_q > 0.5, slopes[h] * base, MASKED))
        q_aug.append(jnp.where(lane < HEAD_DIM, q_heads[h].astype(F32), feat).astype(BF16))
    q_aug = jnp.concatenate(q_aug, axis=0)

    def k_tile(kt):
        return ksel_ref[0, pl.ds(pl.multiple_of(kt * SEL_TILE, SEL_TILE), SEL_TILE), :]

    def v_tile(kt):
        return vsel_ref[0, pl.ds(pl.multiple_of(kt * SEL_TILE, SEL_TILE), SEL_TILE), :]

    s_bufs = (s0_ref, s1_ref)
    p_bufs = (p0_ref, p1_ref)
    qa_ref[...] = q_aug
    m_ref[...] = jnp.full((rows, LANES), NEG, F32)
    acc_ref[...] = jnp.zeros((rows, 2 * LANES), F32)
    p1_ref[...] = jnp.zeros((rows, SEL_TILE), BF16)
    s0_ref[...] = _dot_nt(q_aug, k_tile(tiles_ref[0]))

    o_win_all = []
    for pair in range(HPG // 2):
        for c in range(n_rc):
            rq = c * ROW_CHUNK + lax.broadcasted_iota(jnp.int32, (ROW_CHUNK, WIN_KEYS), 0)
            dw = q0 + rq - start - lax.broadcasted_iota(jnp.int32, (ROW_CHUNK, WIN_KEYS), 1)
            valid_w = (dw | (WINDOW - 1 - dw)) >= 0
            for h in (2 * pair, 2 * pair + 1):
                r0 = h * Q_TILE + c * ROW_CHUNK
                s = jnp.where(valid_w, sw_ref[r0:r0 + ROW_CHUNK, :], NEG)
                e = jnp.exp(s - jnp.max(s, axis=-1, keepdims=True))
                ew_ref[r0:r0 + ROW_CHUNK, :] = e.astype(BF16)
        pair_rows = slice(2 * pair * Q_TILE, (2 * pair + 2) * Q_TILE)
        o_win_all.append(_dot(ew_ref[pair_rows, :], vwin_ref[0, pl.ds(start, WIN_KEYS), :]))
    o_win_all = jnp.concatenate(o_win_all, axis=0)

    def online(s_ref, p_ref, pv, causal_tile, width=SEL_TILE):
        for c in range(n_rc):
            if causal_tile:
                kpos = n_full * SEL_TILE + lax.broadcasted_iota(jnp.int32, (ROW_CHUNK, width), 1)
                tq = q0 + c * ROW_CHUNK + lax.broadcasted_iota(jnp.int32, (ROW_CHUNK, width), 0)
                causal = kpos <= tq
            for h in range(HPG):
                r0 = h * Q_TILE + c * ROW_CHUNK
                rs = slice(r0, r0 + ROW_CHUNK)
                s = s_ref[rs, :width]
                if causal_tile:
                    s = jnp.where(causal, s, MASKED)
                m_old = m_ref[rs, :]
                m_new = jnp.maximum(m_old, jnp.max(s, axis=-1, keepdims=True))
                m_ref[rs, :] = m_new
                p = [jnp.exp(s[:, j * LANES:(j + 1) * LANES] - m_new)
                     for j in range(width // LANES)]
                p_ref[rs, :width] = jnp.concatenate(p, axis=1).astype(BF16)
                alpha = jnp.exp(m_old - m_new)
                acc_ref[rs, :] = jnp.concatenate([alpha, alpha], axis=1) * (acc_ref[rs, :]
                                                                           + pv[rs, :])

    def stage(i, par):
        s_bufs[1 - par][...] = _dot_nt(qa_ref[...], k_tile(tiles_ref[i + 1]))
        pv = _dot(p_bufs[1 - par][...], v_tile(tiles_ref[jnp.maximum(i - 1, 0)]))
        online(s_bufs[par], p_bufs[par], pv, False)

    def sweep(i, carry):
        lax.cond(i % 2 == 0, lambda: stage(i, 0), lambda: stage(i, 1))
        return carry

    lax.fori_loop(0, n_active, sweep, 0)

    def last(par, width):
        pv = _dot(p_bufs[1 - par][...], v_tile(tiles_ref[jnp.maximum(n_active - 1, 0)]))
        online(s_bufs[par], p_bufs[par], pv, True, width)
        own_v = vsel_ref[0, pl.ds(pl.multiple_of(n_full * SEL_TILE, SEL_TILE), width), :]
        acc_ref[...] += _dot(p_bufs[par][:, :width], own_v)

    def finish(width):
        lax.cond(n_active % 2 == 0, lambda: last(0, width), lambda: last(1, width))

    own_last = (ci0 + Q_TILE // SEL_BLOCK - 1) % tile_blocks
    lax.cond(own_last < tile_blocks // 2, lambda: finish(SEL_TILE // 2), lambda: finish(SEL_TILE))
    o_sel_all = acc_ref[...]

    gates = gate_ref[...]
    g_hi = gates.astype(BF16)
    g_lo = (gates - g_hi.astype(F32)).astype(BF16)
    expand = gexp_ref[...]
    g_wide = _dot(g_hi, expand) + _dot(g_lo, expand)
    even_lanes = lane < HEAD_DIM
    sees_cmp = q0 + lax.broadcasted_iota(jnp.int32, (Q_TILE, LANES), 0) >= CMP_BLOCK - 1
    pairs = []
    for pair in range(HPG // 2):
        h0, h1 = 2 * pair, 2 * pair + 1
        def gate(n, pair=pair):
            blk = n * (HPG // 2) + pair
            return g_wide[:, blk * LANES:(blk + 1) * LANES]
        def both(x0, x1):
            return jnp.where(even_lanes, x0, x1)
        o_cmp = both(head_rows(o_cmp_all, h0) * jnp.concatenate(r_cmp[h0], axis=0),
                     head_rows(o_cmp_all, h1) * jnp.concatenate(r_cmp[h1], axis=0))
        o_cmp = jnp.where(sees_cmp, o_cmp, 0.0)
        def normalised(acc_all):
            a0, a1 = head_rows(acc_all, h0), head_rows(acc_all, h1)
            return both(a0[:, :LANES], a1[:, LANES:]) * (1.0 / both(a0[:, LANES:], a1[:, :LANES]))
        pairs.append(gate(0) * o_cmp + gate(1) * normalised(o_sel_all)
                     + gate(2) * normalised(o_win_all))
    o_ref[...] = jnp.concatenate(pairs, axis=1).astype(BF16)


def _gate_expand():
    col = jnp.arange(N_KV * N_BRANCH * (HPG // 2) * LANES)
    blk, h_in_pair = col // LANES, (col % LANES) // HEAD_DIM
    g, n, pair = blk // (N_BRANCH * HPG // 2), (blk // (HPG // 2)) % N_BRANCH, blk % (HPG // 2)
    row = jnp.arange(LANES)[:, None]
    return (row == (n * N_HEADS + g * HPG + 2 * pair + h_in_pair)[None, :]).astype(BF16)


def _attention(q_pad, kvc, k4, v4, gates, ovt, batch, seq):
    n_steps = seq // Q_TILE
    rows = HPG * Q_TILE
    n_cmp = kvc.shape[2]
    n_sel = ovt.shape[0]
    k_spec = lambda branch: pl.BlockSpec(
        (1, seq, LANES), lambda b, g, c, branch=branch: (branch * N_KV + g, b, 0))
    v_spec = lambda branch: pl.BlockSpec(
        (1, seq, 2 * LANES), lambda b, g, c, branch=branch: (branch * N_KV + g, b, 0))
    return pl.pallas_call(
        _attn_kernel,
        grid=(batch, N_KV, n_steps),
        in_specs=[
            pl.BlockSpec((Q_TILE, HPG * LANES), lambda b, g, c: (b * n_steps + c, g)),
            pl.BlockSpec((1, 1, n_cmp, LANES), lambda b, g, c: (g, b, 0, 0)),
            pl.BlockSpec((1, 1, n_cmp, LANES), lambda b, g, c: (N_KV + g, b, 0, 0)),
            k_spec(0), v_spec(0), k_spec(1), v_spec(1),
            pl.BlockSpec((Q_TILE, LANES), lambda b, g, c: (b * n_steps + c, 0)),
            pl.BlockSpec((n_sel, n_cmp), lambda b, g, c: (0, 0)),
            pl.BlockSpec((LANES, N_BRANCH * (HPG // 2) * LANES), lambda b, g, c: (0, g)),
        ],
        out_specs=pl.BlockSpec((Q_TILE, HPG * HEAD_DIM), lambda b, g, c: (b * n_steps + c, g)),
        out_shape=jax.ShapeDtypeStruct((batch * seq, D_ATTN), BF16),
        scratch_shapes=[
            pltpu.VMEM((rows, LANES), BF16),
            pltpu.VMEM((rows, LANES), F32),
            pltpu.VMEM((rows, 2 * LANES), F32),
            pltpu.VMEM((rows, SEL_TILE), F32), pltpu.VMEM((rows, SEL_TILE), F32),
            pltpu.VMEM((rows, SEL_TILE), BF16), pltpu.VMEM((rows, SEL_TILE), BF16),
            pltpu.VMEM((rows, WIN_KEYS), F32), pltpu.VMEM((rows, WIN_KEYS), BF16),
            pltpu.VMEM((rows, n_cmp), F32), pltpu.VMEM((rows, n_cmp), BF16),
            pltpu.SMEM((seq // SEL_TILE + 1,), jnp.int32),
        ],
        compiler_params=pltpu.CompilerParams(
            dimension_semantics=("arbitrary", "arbitrary", "arbitrary"),
            vmem_limit_bytes=VMEM_LIMIT),
        name="attn",
    )(q_pad, kvc, kvc, k4, v4, k4, v4, gates, ovt, _gate_expand())


def _rms(x, g):
    return x * lax.rsqrt(jnp.mean(x * x, axis=-1, keepdims=True) + EPS) * g


def _out_ffn_kernel(x_ref, u0_ref, un_ref, a_ref, cw_ref, cb_ref, cg_ref, cbeta_ref,
                    wc_ref, wa_ref, g2_ref, w1_ref, w2_ref, gf_ref, o_ref,
                    conv_ref, ext_ref, sh_ref, *, seq):
    i = pl.program_id(0)
    tm = x_ref.shape[0]
    conv_args = (cw_ref, cb_ref, cg_ref, cbeta_ref, conv_ref, ext_ref, sh_ref)

    @pl.when(i == 0)
    def _():
        ext_ref[...] = jnp.zeros(ext_ref.shape, F32)
        _conv_tile(u0_ref[...], True, *conv_args)

    conv_cur = conv_ref[...]
    _conv_tile(un_ref[...], ((i + 1) * tm) % seq == 0, *conv_args)
    x1 = x_ref[...] + _dot(conv_cur, wc_ref[...]) + _dot(a_ref[...], wa_ref[...])
    h2 = _rms(x1, g2_ref[...]).astype(BF16)
    a = jnp.maximum(_dot(h2, w1_ref[...]), 0.0)
    y = x1 + _dot((a * a).astype(BF16), w2_ref[...])
    o_ref[...] = _rms(y, gf_ref[...])


def _out_ffn(x2, u, attn_out, conv_w, conv_b, cln_g, cln_b, wc, wa, g2, w1, w2, gf, seq):
    n_tok = x2.shape[0]
    n_tiles = n_tok // TM_FFN
    const = lambda shape: pl.BlockSpec(shape, lambda i: (0, 0), pipeline_mode=pl.Buffered(1))
    return pl.pallas_call(
        functools.partial(_out_ffn_kernel, seq=seq),
        grid=(n_tiles,),
        in_specs=[
            pl.BlockSpec((TM_FFN, D_MODEL), lambda i: (i, 0)),
            const((TM_FFN, D_CONV)),
            pl.BlockSpec((TM_FFN, D_CONV), lambda i: (jnp.minimum(i + 1, n_tiles - 1), 0)),
            pl.BlockSpec((TM_FFN, D_ATTN), lambda i: (i, 0)),
            const((CONV_WIDTH * SUBLANES, D_CONV)),
            const((1, D_CONV)), const((1, D_CONV)), const((1, D_CONV)),
            const((D_CONV, D_MODEL)),
            const((D_ATTN, D_MODEL)),
            const((1, D_MODEL)),
            const((D_MODEL, D_FF)),
            const((D_FF, D_MODEL)),
            const((1, D_MODEL)),
        ],
        out_specs=pl.BlockSpec((TM_FFN, D_MODEL), lambda i: (i, 0)),
        out_shape=jax.ShapeDtypeStruct((n_tok, D_MODEL), F32),
        scratch_shapes=[
            pltpu.VMEM((TM_FFN, D_CONV), BF16),
            pltpu.VMEM((TM_FFN + CONV_HALO, D_CONV), F32),
            pltpu.VMEM((SUBLANES - 1, TM_FFN + CONV_HALO - SUBLANES, D_CONV), F32),
        ],
        compiler_params=pltpu.CompilerParams(
            dimension_semantics=("arbitrary",), vmem_limit_bytes=VMEM_LIMIT),
        name="out_ffn",
    )(x2, u, u, attn_out, conv_w, conv_b, cln_g, cln_b, wc, wa, g2, w1, w2, gf)


def _pack_w_in(w_in):
    o2 = 2 * D_CONV
    o3 = o2 + D_ATTN
    o4 = o3 + 2 * N_BRANCH * N_KV * HEAD_DIM
    gate_w = w_in[:, o4:].reshape(D_MODEL, N_HEADS, N_BRANCH).transpose(0, 2, 1)
    gate_w = gate_w.reshape(D_MODEL, N_BRANCH * N_HEADS)
    cols = [w_in[:, :o2], w_in[:, o2:o3] * HEAD_DIM ** -0.5, w_in[:, o3:o4], gate_w,
            jnp.zeros((D_MODEL, LANES - N_BRANCH * N_HEADS), w_in.dtype)]
    return jnp.concatenate(cols, axis=1).astype(BF16)


def _overlap_t(n_cmp_rows, n_sel):
    c = jnp.arange(n_cmp_rows)[None, :]
    j = jnp.arange(n_sel)[:, None]
    ov = (c * CMP_STRIDE < (j + 1) * SEL_BLOCK) & (c * CMP_STRIDE + CMP_BLOCK > j * SEL_BLOCK)
    return ov.astype(BF16)


def kernel(x, norm1_g, w_in, dw_w, dw_b, cln_g, cln_b, ck_pe, ck_w1, ck_w2, cv_pe, cv_w1, cv_w2,
           w_out, norm2_g, w_ff1, w_ff2, norm_f_g):
    batch, seq, _ = x.shape
    n_tok = batch * seq
    x2 = x.reshape(n_tok, D_MODEL)
    assert norm1_g.shape[0] == 1, "the final norm is fused into the (single) layer's last kernel"
    for l in range(1):
        w_a = _pack_w_in(w_in[l])
        u, q_pad, kvc_raw, k4, v4, gates = _proj_in(x2, norm1_g[l][None, :], w_a, seq)

        n_rows = seq // CMP_STRIDE
        a4 = kvc_raw.reshape(4, batch, seq, HEAD_DIM)
        pes = jnp.stack([ck_pe[l], cv_pe[l]]).reshape(2, 1, CMP_BLOCK * HEAD_DIM)
        w1s = jnp.stack([ck_w1[l], cv_w1[l]]).astype(BF16)
        zero2 = jnp.zeros_like(ck_w2[l])
        w2s = jnp.stack([jnp.concatenate([ck_w2[l], zero2], axis=1),
                         jnp.concatenate([cv_w2[l], cv_w2[l]], axis=1)]).astype(BF16)
        kvc = _compress(a4, pes, w1s, w2s)

        ovt = _overlap_t(n_rows, seq // SEL_BLOCK)
        attn_out = _attention(q_pad, kvc, k4, v4, gates, ovt, batch, seq)

        wo = w_out[l].astype(BF16)
        conv_w = jnp.repeat(dw_w[l][:, 0, :], SUBLANES, axis=0)
        x2 = _out_ffn(x2, u, attn_out, conv_w, dw_b[l][None, :], cln_g[l][None, :],
                      cln_b[l][None, :], wo[:D_CONV], wo[D_CONV:], norm2_g[l][None, :],
                      w_ff1[l].astype(BF16), w_ff2[l].astype(BF16), norm_f_g[None, :], seq)
    return x2.reshape(batch, seq, D_MODEL)
```

```python
import functools

import jax
import jax.numpy as jnp
from jax import lax
from jax.experimental import pallas as pl
from jax.experimental.pallas import tpu as pltpu

D_MODEL = 1024
D_CONV = 512
CONV_WIDTH = 31
N_HEADS = 8
HEAD_DIM = 64
N_KV = 2
HPG = N_HEADS // N_KV
D_ATTN = N_HEADS * HEAD_DIM
N_BRANCH = 3
CMP_BLOCK = 32
CMP_STRIDE = 16
CMP_HIDDEN = 256
SEL_BLOCK = 64
SEL_SHIFT = 6
SEL_TOP = 16
WINDOW = 512
D_FF = 4 * D_MODEL
EPS = 1e-6
NEG = -1e30
FORCE = 1e30
MASKED = -1e32

LANES = 128
SUBLANES = 8
F32 = jnp.float32
BF16 = jnp.bfloat16

TM_PROJ = 1024
CONV_HALO = 32
CONV_ROWS = 16
TM_FFN = 512
SEL_TILE = 512
Q_TILE = 256
ROW_CHUNK = 64
RANK_WAYS = 1
F_WIN_BLK = HEAD_DIM
F_WIN_OFF = HEAD_DIM + 1
F_WIN_ONE = HEAD_DIM + 2
F_CMP_IDX = HEAD_DIM + 3
WIN_KEYS = WINDOW + Q_TILE
VMEM_LIMIT = 56 * 1024 * 1024

C_U = 0
C_Q = 2 * D_CONV
C_KVC = C_Q + D_ATTN
C_KV = C_KVC + 4 * HEAD_DIM
C_GATE = C_KV + 8 * HEAD_DIM
C_END = C_GATE + LANES

NT_DIMS = (((1,), (1,)), ((), ()))


def _dot(a, b):
    return jnp.dot(a, b, preferred_element_type=F32)


def _dot_nt(a, b):
    return lax.dot_general(a, b, NT_DIMS, preferred_element_type=F32)


def _proj_in_kernel(x_ref, g_ref, w_ref, u_ref, q_ref, kvc_ref, k_ref, v_ref, gate_ref, *, seq):
    i = pl.program_id(0)
    x = x_ref[...]
    ms = jnp.mean(x * x, axis=-1, keepdims=True)
    h = (x * lax.rsqrt(ms + EPS) * g_ref[...]).astype(BF16)

    zu = _dot(h, w_ref[:, C_U:C_Q])
    u_ref[...] = zu[:, :D_CONV] * jax.nn.sigmoid(zu[:, D_CONV:])

    tm = x.shape[0]
    pad64 = jnp.zeros((tm, HEAD_DIM), F32)

    def emit_q():
        zq = _dot(h, w_ref[:, C_Q:C_KVC])
        for hd in range(N_HEADS):
            q_ref[:, hd * LANES:(hd + 1) * LANES] = jnp.concatenate(
                [zq[:, hd * HEAD_DIM:(hd + 1) * HEAD_DIM], pad64], axis=1).astype(BF16)

    def emit_kvc():
        zc = _dot(h, w_ref[:, C_KVC:C_KV])
        for n in range(4):
            kvc_ref[n] = zc[:, n * HEAD_DIM:(n + 1) * HEAD_DIM]

    def emit_kv():
        t = (i * tm) % seq + lax.broadcasted_iota(jnp.int32, (tm, LANES), 0)
        lane = lax.broadcasted_iota(jnp.int32, (tm, LANES), 1)
        feat = jnp.where(lane == LANES - 1, (t & (SEL_BLOCK - 1)).astype(F32),
                         jnp.where(lane - HEAD_DIM == (t >> SEL_SHIFT), 1.0, 0.0))
        win_feat = jnp.where(lane == F_WIN_BLK, (t >> SEL_SHIFT).astype(F32),
                             jnp.where(lane == F_WIN_OFF, (t & (SEL_BLOCK - 1)).astype(F32),
                                       jnp.where(lane == F_WIN_ONE, 1.0, 0.0)))
        ones64 = jnp.ones((tm, HEAD_DIM), F32)
        zkv = _dot(h, w_ref[:, C_KV:C_GATE])
        for n in range(8):
            z = zkv[:, n * HEAD_DIM:(n + 1) * HEAD_DIM]
            kind, grp = divmod(n, N_KV)
            if kind % 2 == 0:
                k = jnp.concatenate([z, pad64], axis=1) + (feat if kind == 0 else win_feat)
                k_ref[kind + grp] = k.astype(BF16)
            else:
                v_ref[kind - 1 + grp] = jnp.concatenate([z, ones64, ones64, z], axis=1).astype(BF16)

    emit_q()
    emit_kvc()
    emit_kv()
    gate_ref[...] = jax.nn.sigmoid(_dot(h, w_ref[:, C_GATE:C_END]))


def _proj_in(x2, g1, w_a, seq):
    n_tok = x2.shape[0]
    grid = (n_tok // TM_PROJ,)
    return pl.pallas_call(
        functools.partial(_proj_in_kernel, seq=seq),
        grid=grid,
        in_specs=[
            pl.BlockSpec((TM_PROJ, D_MODEL), lambda i: (i, 0)),
            pl.BlockSpec((1, D_MODEL), lambda i: (0, 0)),
            pl.BlockSpec((D_MODEL, C_END), lambda i: (0, 0), pipeline_mode=pl.Buffered(1)),
        ],
        out_specs=[
            pl.BlockSpec((TM_PROJ, D_CONV), lambda i: (i, 0)),
            pl.BlockSpec((TM_PROJ, N_HEADS * LANES), lambda i: (i, 0)),
            pl.BlockSpec((4, TM_PROJ, HEAD_DIM), lambda i: (0, i, 0)),
            pl.BlockSpec((4, TM_PROJ, LANES), lambda i: (0, i, 0)),
            pl.BlockSpec((4, TM_PROJ, 2 * LANES), lambda i: (0, i, 0)),
            pl.BlockSpec((TM_PROJ, LANES), lambda i: (i, 0)),
        ],
        out_shape=[
            jax.ShapeDtypeStruct((n_tok, D_CONV), F32),
            jax.ShapeDtypeStruct((n_tok, N_HEADS * LANES), BF16),
            jax.ShapeDtypeStruct((4, n_tok, HEAD_DIM), F32),
            jax.ShapeDtypeStruct((4, n_tok, LANES), BF16),
            jax.ShapeDtypeStruct((4, n_tok, 2 * LANES), BF16),
            jax.ShapeDtypeStruct((n_tok, LANES), F32),
        ],
        compiler_params=pltpu.CompilerParams(
            dimension_semantics=("arbitrary",), vmem_limit_bytes=VMEM_LIMIT),
        name="proj_in",
    )(x2, g1, w_a)


def _compress_kernel(a_ref, pe_ref, w1_ref, w2_ref, o_ref):
    n_rows = a_ref.shape[2] // CMP_STRIDE
    a = jnp.concatenate(
        [a_ref[0, 0, pl.ds(i, n_rows, stride=CMP_STRIDE), :] for i in range(CMP_STRIDE)],
        axis=1).astype(BF16)
    half = CMP_STRIDE * HEAD_DIM
    top = _dot(a, w1_ref[0, :half, :])
    bot = _dot(a, w1_ref[0, half:, :])
    bot = pltpu.roll(bot, n_rows - 1, 0)
    pe = _dot(pe_ref[0].astype(BF16), w1_ref[0])
    hid = jax.nn.gelu(top + bot + pe, approximate=True)
    out = _dot(hid.astype(BF16), w2_ref[0])
    row = lax.broadcasted_iota(jnp.int32, out.shape, 0)
    lane = lax.broadcasted_iota(jnp.int32, out.shape, 1)
    is_key = pl.program_id(0) < N_KV
    out = out + jnp.where((lane == F_CMP_IDX) & is_key, row.astype(F32), 0.0)
    o_ref[0, 0] = jnp.where(row == n_rows - 1, 0.0, out).astype(BF16)


def _compress(a4, pes, w1s, w2s):
    _, batch, seq, _ = a4.shape
    n_rows = seq // CMP_STRIDE
    return pl.pallas_call(
        _compress_kernel,
        grid=(4, batch),
        in_specs=[
            pl.BlockSpec((1, 1, seq, HEAD_DIM), lambda n, b: (n, b, 0, 0)),
            pl.BlockSpec((1, 1, CMP_BLOCK * HEAD_DIM), lambda n, b: (n // N_KV, 0, 0)),
            pl.BlockSpec((1, CMP_BLOCK * HEAD_DIM, CMP_HIDDEN), lambda n, b: (n // N_KV, 0, 0)),
            pl.BlockSpec((1, CMP_HIDDEN, LANES), lambda n, b: (n // N_KV, 0, 0)),
        ],
        out_specs=pl.BlockSpec((1, 1, n_rows, LANES), lambda n, b: (n, b, 0, 0)),
        out_shape=jax.ShapeDtypeStruct((4, batch, n_rows, LANES), BF16),
        compiler_params=pltpu.CompilerParams(
            dimension_semantics=("arbitrary", "arbitrary"), vmem_limit_bytes=VMEM_LIMIT),
        name="compress",
    )(a4, pes, w1s, w2s)


def _conv_tile(u, first, w_ref, b_ref, g_ref, beta_ref, o_ref, ext_ref, sh_ref):
    tm = u.shape[0]
    ext_ref[0:CONV_HALO, :] = jnp.where(first, 0.0, ext_ref[tm:tm + CONV_HALO, :])
    ext_ref[CONV_HALO:, :] = u
    off = CONV_HALO - (CONV_WIDTH - 1)
    n_sh = sh_ref.shape[1]
    for b in range(1, SUBLANES):
        sh_ref[b - 1] = ext_ref[b:b + n_sh, :]
    for r in range(tm // CONV_ROWS):
        r0 = r * CONV_ROWS
        acc = jnp.zeros((CONV_ROWS, D_CONV), F32) + b_ref[...]
        for k in range(CONV_WIDTH):
            a, b = divmod(off + k, SUBLANES)
            lo = r0 + a * SUBLANES
            tap = ext_ref[lo:lo + CONV_ROWS, :] if b == 0 else sh_ref[b - 1, lo:lo + CONV_ROWS, :]
            w_k = w_ref[k * SUBLANES:(k + 1) * SUBLANES, :]
            acc = acc + tap * jnp.concatenate([w_k] * (CONV_ROWS // SUBLANES), axis=0)
        mu = jnp.mean(acc, axis=-1, keepdims=True)
        d = acc - mu
        var = jnp.mean(d * d, axis=-1, keepdims=True)
        y = d * lax.rsqrt(var + EPS) * g_ref[...] + beta_ref[...]
        o_ref[r0:r0 + CONV_ROWS, :] = (y * jax.nn.sigmoid(y)).astype(BF16)


def _exp_rows(s):
    e = jnp.exp(s - jnp.max(s, axis=-1, keepdims=True))
    return e, 1.0 / jnp.sum(e, axis=-1, keepdims=True)


def _attn_kernel(q_ref, kc_ref, vc_ref, ksel_ref, vsel_ref, kwin_ref, vwin_ref,
                 gate_ref, ovt_ref, gexp_ref, o_ref,
                 qa_ref, m_ref, acc_ref, s0_ref, s1_ref, p0_ref, p1_ref,
                 sw_ref, ew_ref, sc_ref, ec_ref, tiles_ref):
    g = pl.program_id(1)
    step = pl.program_id(2)
    q0 = step * Q_TILE
    ci0 = step * (Q_TILE // SEL_BLOCK)
    rows = HPG * Q_TILE
    qb = q_ref[...]
    q_heads = [qb[:, h * LANES:(h + 1) * LANES] for h in range(HPG)]
    slope0 = jnp.where(g == 0, 0.5, 0.5 ** (HPG + 1)).astype(F32)
    slopes = [slope0 * (0.5 ** h) for h in range(HPG)]
    head_rows = lambda a, h: a[h * Q_TILE:(h + 1) * Q_TILE]

    lane = lax.broadcasted_iota(jnp.int32, (Q_TILE, LANES), 1)
    n_rc = Q_TILE // ROW_CHUNK


    win_blk0 = jnp.maximum(ci0 - WINDOW // SEL_BLOCK, 0)
    start = pl.multiple_of(win_blk0 * SEL_BLOCK, SEL_BLOCK)
    q_cw = []
    for h in range(HPG):
        feat = jnp.where(
            lane == F_WIN_BLK, SEL_BLOCK * slopes[h],
            jnp.where(lane == F_WIN_OFF, slopes[h],
                      jnp.where(lane == F_WIN_ONE, -SEL_BLOCK * slopes[h] * win_blk0.astype(F32),
                                jnp.where(lane == F_CMP_IDX, CMP_STRIDE * slopes[h], 0.0))))
        q_cw.append(jnp.where(lane < HEAD_DIM, q_heads[h].astype(F32), feat).astype(BF16))
    q_cw = jnp.concatenate(q_cw, axis=0)

    n_cmp = kc_ref.shape[2]
    sc_ref[...] = _dot_nt(q_cw, kc_ref[0, 0])
    r_cmp = [[None] * n_rc for _ in range(HPG)]
    p_sum = []
    for c in range(n_rc):
        rq = c * ROW_CHUNK + lax.broadcasted_iota(jnp.int32, (ROW_CHUNK, n_cmp), 0)
        cc = lax.broadcasted_iota(jnp.int32, (ROW_CHUNK, n_cmp), 1)
        valid = q0 + rq - cc * CMP_STRIDE >= CMP_BLOCK - 1
        p_chunk = jnp.zeros((ROW_CHUNK, n_cmp), F32)
        for h in range(HPG):
            r0 = h * Q_TILE + c * ROW_CHUNK
            e, r_sum = _exp_rows(jnp.where(valid, sc_ref[r0:r0 + ROW_CHUNK, :], NEG))
            p_chunk = p_chunk + e * r_sum
            ec_ref[r0:r0 + ROW_CHUNK, :] = e.astype(BF16)
            r_cmp[h][c] = r_sum
        p_sum.append(p_chunk)
    p_sum = jnp.concatenate(p_sum, axis=0)
    o_cmp_all = _dot(ec_ref[...], vc_ref[0, 0])

    p_hi = p_sum.astype(BF16)
    p_lo = (p_sum - p_hi.astype(F32)).astype(BF16)
    ovt = ovt_ref[...]
    imp = _dot_nt(ovt, p_hi) + _dot_nt(ovt, p_lo)
    n_sel = imp.shape[0]
    jrow = lax.broadcasted_iota(jnp.int32, (n_sel, Q_TILE), 0)
    qlane = lax.broadcasted_iota(jnp.int32, (n_sel, Q_TILE), 1)
    cl = ci0 + (qlane >> SEL_SHIFT)
    forced = (jrow == 0) | (jrow == cl) | (jrow == cl - 1)
    imp = jnp.where(forced, FORCE, jnp.where(jrow > cl, NEG, imp))
    sw_ref[...] = _dot_nt(q_cw, kwin_ref[0, pl.ds(start, WIN_KEYS), :])
    groups = [imp[8 * v:8 * v + 8, :] for v in range(n_sel // 8)]
    sub = lax.broadcasted_iota(jnp.int32, (8, Q_TILE), 0)
    ranks = [[jnp.zeros((8, Q_TILE), jnp.int32) for _ in range(RANK_WAYS)] for _ in groups]
    for jp in range(n_sel):
        row = groups[jp // 8][jp % 8:jp % 8 + 1, :]
        w = jp % RANK_WAYS
        for v in range(n_sel // 8):
            x = groups[v]
            if 8 * v + 7 < jp:
                ranks[v][w] = ranks[v][w] + jnp.where(row > x, 1, 0)
            elif 8 * v > jp:
                ranks[v][w] = ranks[v][w] + jnp.where(row >= x, 1, 0)
            else:
                tie = jnp.where(sub + 8 * v > jp, 1, 0)
                ranks[v][w] = (ranks[v][w] + jnp.where(row > x, 1, 0)
                               + jnp.where(row == x, tie, 0))
    rank = jnp.concatenate([sum(r[1:], r[0]) for r in ranks], axis=0)
    sel_t = jnp.where((rank < SEL_TOP) & (jrow <= cl), 1.0, 0.0)

    tile_blocks = SEL_TILE // SEL_BLOCK
    n_full = (ci0 + Q_TILE // SEL_BLOCK - 1) // tile_blocks
    n_active = jnp.int32(0)
    for t in range(n_sel // tile_blocks):
        wanted = jnp.max(sel_t[t * tile_blocks:(t + 1) * tile_blocks, :]) > 0.5
        tiles_ref[n_active] = t
        n_active = n_active + jnp.where(wanted & (t < n_full), 1, 0)
    tiles_ref[n_active] = n_full

    sel_t = jnp.concatenate([jnp.zeros((LANES - n_sel, Q_TILE), F32), sel_t], axis=0)
    sel_q = jnp.concatenate(
        [sel_t[:, c * LANES:(c + 1) * LANES].T for c in range(Q_TILE // LANES)], axis=0)
    qrow = lax.broadcasted_iota(jnp.int32, (Q_TILE, LANES), 0)
    base = ((lane - HEAD_DIM - ci0 - (qrow >> SEL_SHIFT)) * SEL_BLOCK).astype(F32)
    q_aug = []
    for h in range(HPG):
        feat = jnp.where(lane == LANES - 1, slopes[h],
                         jnp.where(sel_q > 0.5, slopes[h] * base, MASKED))
        q_aug.append(jnp.where(lane < HEAD_DIM, q_heads[h].astype(F32), feat).astype(BF16))
    q_aug = jnp.concatenate(q_aug, axis=0)

    def k_tile(kt):
        return ksel_ref[0, pl.ds(pl.multiple_of(kt * SEL_TILE, SEL_TILE), SEL_TILE), :]

    def v_tile(kt):
        return vsel_ref[0, pl.ds(pl.multiple_of(kt * SEL_TILE, SEL_TILE), SEL_TILE), :]

    s_bufs = (s0_ref, s1_ref)
    p_bufs = (p0_ref, p1_ref)
    qa_ref[...] = q_aug
    m_ref[...] = jnp.full((rows, LANES), NEG, F32)
    acc_ref[...] = jnp.zeros((rows, 2 * LANES), F32)
    p1_ref[...] = jnp.zeros((rows, SEL_TILE), BF16)
    s0_ref[...] = _dot_nt(q_aug, k_tile(tiles_ref[0]))

    o_win_all = []
    for pair in range(HPG // 2):
        for c in range(n_rc):
            rq = c * ROW_CHUNK + lax.broadcasted_iota(jnp.int32, (ROW_CHUNK, WIN_KEYS), 0)
            dw = q0 + rq - start - lax.broadcasted_iota(jnp.int32, (ROW_CHUNK, WIN_KEYS), 1)
            valid_w = (dw | (WINDOW - 1 - dw)) >= 0
            for h in (2 * pair, 2 * pair + 1):
                r0 = h * Q_TILE + c * ROW_CHUNK
                s = jnp.where(valid_w, sw_ref[r0:r0 + ROW_CHUNK, :], NEG)
                e = jnp.exp(s - jnp.max(s, axis=-1, keepdims=True))
                ew_ref[r0:r0 + ROW_CHUNK, :] = e.astype(BF16)
        pair_rows = slice(2 * pair * Q_TILE, (2 * pair + 2) * Q_TILE)
        o_win_all.append(_dot(ew_ref[pair_rows, :], vwin_ref[0, pl.ds(start, WIN_KEYS), :]))
    o_win_all = jnp.concatenate(o_win_all, axis=0)

    def online(s_ref, p_ref, pv, causal_tile, width=SEL_TILE):
        for c in range(n_rc):
            if causal_tile:
                kpos = n_full * SEL_TILE + lax.broadcasted_iota(jnp.int32, (ROW_CHUNK, width), 1)
                tq = q0 + c * ROW_CHUNK + lax.broadcasted_iota(jnp.int32, (ROW_CHUNK, width), 0)
                causal = kpos <= tq
            for h in range(HPG):
                r0 = h * Q_TILE + c * ROW_CHUNK
                rs = slice(r0, r0 + ROW_CHUNK)
                s = s_ref[rs, :width]
                if causal_tile:
                    s = jnp.where(causal, s, MASKED)
                m_old = m_ref[rs, :]
                m_new = jnp.maximum(m_old, jnp.max(s, axis=-1, keepdims=True))
                m_ref[rs, :] = m_new
                p = [jnp.exp(s[:, j * LANES:(j + 1) * LANES] - m_new)
                     for j in range(width // LANES)]
                p_ref[rs, :width] = jnp.concatenate(p, axis=1).astype(BF16)
                alpha = jnp.exp(m_old - m_new)
                acc_ref[rs, :] = jnp.concatenate([alpha, alpha], axis=1) * (acc_ref[rs, :]
                                                                           + pv[rs, :])

    def stage(i, par):
        s_bufs[1 - par][...] = _dot_nt(qa_ref[...], k_tile(tiles_ref[i + 1]))
        pv = _dot(p_bufs[1 - par][...], v_tile(tiles_ref[jnp.maximum(i - 1, 0)]))
        online(s_bufs[par], p_bufs[par], pv, False)

    def sweep(i, carry):
        lax.cond(i % 2 == 0, lambda: stage(i, 0), lambda: stage(i, 1))
        return carry

    lax.fori_loop(0, n_active, sweep, 0)

    def last(par, width):
        pv = _dot(p_bufs[1 - par][...], v_tile(tiles_ref[jnp.maximum(n_active - 1, 0)]))
        online(s_bufs[par], p_bufs[par], pv, True, width)
        own_v = vsel_ref[0, pl.ds(pl.multiple_of(n_full * SEL_TILE, SEL_TILE), width), :]
        acc_ref[...] += _dot(p_bufs[par][:, :width], own_v)

    def finish(width):
        lax.cond(n_active % 2 == 0, lambda: last(0, width), lambda: last(1, width))

    own_last = (ci0 + Q_TILE // SEL_BLOCK - 1) % tile_blocks
    lax.cond(own_last < tile_blocks // 2, lambda: finish(SEL_TILE // 2), lambda: finish(SEL_TILE))
    o_sel_all = acc_ref[...]

    gates = gate_ref[...]
    g_hi = gates.astype(BF16)
    g_lo = (gates - g_hi.astype(F32)).astype(BF16)
    expand = gexp_ref[...]
    g_wide = _dot(g_hi, expand) + _dot(g_lo, expand)
    even_lanes = lane < HEAD_DIM
    sees_cmp = q0 + lax.broadcasted_iota(jnp.int32, (Q_TILE, LANES), 0) >= CMP_BLOCK - 1
    pairs = []
    for pair in range(HPG // 2):
        h0, h1 = 2 * pair, 2 * pair + 1
        def gate(n, pair=pair):
            blk = n * (HPG // 2) + pair
            return g_wide[:, blk * LANES:(blk + 1) * LANES]
        def both(x0, x1):
            return jnp.where(even_lanes, x0, x1)
        o_cmp = both(head_rows(o_cmp_all, h0) * jnp.concatenate(r_cmp[h0], axis=0),
                     head_rows(o_cmp_all, h1) * jnp.concatenate(r_cmp[h1], axis=0))
        o_cmp = jnp.where(sees_cmp, o_cmp, 0.0)
        def normalised(acc_all):
            a0, a1 = head_rows(acc_all, h0), head_rows(acc_all, h1)
            return both(a0[:, :LANES], a1[:, LANES:]) * (1.0 / both(a0[:, LANES:], a1[:, :LANES]))
        pairs.append(gate(0) * o_cmp + gate(1) * normalised(o_sel_all)
                     + gate(2) * normalised(o_win_all))
    o_ref[...] = jnp.concatenate(pairs, axis=1).astype(BF16)


def _gate_expand():
    col = jnp.arange(N_KV * N_BRANCH * (HPG // 2) * LANES)
    blk, h_in_pair = col // LANES, (col % LANES) // HEAD_DIM
    g, n, pair = blk // (N_BRANCH * HPG // 2), (blk // (HPG // 2)) % N_BRANCH, blk % (HPG // 2)
    row = jnp.arange(LANES)[:, None]
    return (row == (n * N_HEADS + g * HPG + 2 * pair + h_in_pair)[None, :]).astype(BF16)


def _attention(q_pad, kvc, k4, v4, gates, ovt, batch, seq):
    n_steps = seq // Q_TILE
    rows = HPG * Q_TILE
    n_cmp = kvc.shape[2]
    n_sel = ovt.shape[0]
    k_spec = lambda branch: pl.BlockSpec(
        (1, seq, LANES), lambda b, g, c, branch=branch: (branch * N_KV + g, b, 0))
    v_spec = lambda branch: pl.BlockSpec(
        (1, seq, 2 * LANES), lambda b, g, c, branch=branch: (branch * N_KV + g, b, 0))
    return pl.pallas_call(
        _attn_kernel,
        grid=(batch, N_KV, n_steps),
        in_specs=[
            pl.BlockSpec((Q_TILE, HPG * LANES), lambda b, g, c: (b * n_steps + c, g)),
            pl.BlockSpec((1, 1, n_cmp, LANES), lambda b, g, c: (g, b, 0, 0)),
            pl.BlockSpec((1, 1, n_cmp, LANES), lambda b, g, c: (N_KV + g, b, 0, 0)),
            k_spec(0), v_spec(0), k_spec(1), v_spec(1),
            pl.BlockSpec((Q_TILE, LANES), lambda b, g, c: (b * n_steps + c, 0)),
            pl.BlockSpec((n_sel, n_cmp), lambda b, g, c: (0, 0)),
            pl.BlockSpec((LANES, N_BRANCH * (HPG // 2) * LANES), lambda b, g, c: (0, g)),
        ],
        out_specs=pl.BlockSpec((Q_TILE, HPG * HEAD_DIM), lambda b, g, c: (b * n_steps + c, g)),
        out_shape=jax.ShapeDtypeStruct((batch * seq, D_ATTN), BF16),
        scratch_shapes=[
            pltpu.VMEM((rows, LANES), BF16),
            pltpu.VMEM((rows, LANES), F32),
            pltpu.VMEM((rows, 2 * LANES), F32),
            pltpu.VMEM((rows, SEL_TILE), F32), pltpu.VMEM((rows, SEL_TILE), F32),
            pltpu.VMEM((rows, SEL_TILE), BF16), pltpu.VMEM((rows, SEL_TILE), BF16),
            pltpu.VMEM((rows, WIN_KEYS), F32), pltpu.VMEM((rows, WIN_KEYS), BF16),
            pltpu.VMEM((rows, n_cmp), F32), pltpu.VMEM((rows, n_cmp), BF16),
            pltpu.SMEM((seq // SEL_TILE + 1,), jnp.int32),
        ],
        compiler_params=pltpu.CompilerParams(
            dimension_semantics=("arbitrary", "arbitrary", "arbitrary"),
            vmem_limit_bytes=VMEM_LIMIT),
        name="attn",
    )(q_pad, kvc, kvc, k4, v4, k4, v4, gates, ovt, _gate_expand())


def _rms(x, g):
    return x * lax.rsqrt(jnp.mean(x * x, axis=-1, keepdims=True) + EPS) * g


def _out_ffn_kernel(x_ref, u0_ref, un_ref, a_ref, cw_ref, cb_ref, cg_ref, cbeta_ref,
                    wc_ref, wa_ref, g2_ref, w1_ref, w2_ref, gf_ref, o_ref,
                    conv_ref, ext_ref, sh_ref, *, seq):
    i = pl.program_id(0)
    tm = x_ref.shape[0]
    conv_args = (cw_ref, cb_ref, cg_ref, cbeta_ref, conv_ref, ext_ref, sh_ref)

    @pl.when(i == 0)
    def _():
        ext_ref[...] = jnp.zeros(ext_ref.shape, F32)
        _conv_tile(u0_ref[...], True, *conv_args)

    conv_cur = conv_ref[...]
    _conv_tile(un_ref[...], ((i + 1) * tm) % seq == 0, *conv_args)
    x1 = x_ref[...] + _dot(conv_cur, wc_ref[...]) + _dot(a_ref[...], wa_ref[...])
    h2 = _rms(x1, g2_ref[...]).astype(BF16)
    a = jnp.maximum(_dot(h2, w1_ref[...]), 0.0)
    y = x1 + _dot((a * a).astype(BF16), w2_ref[...])
    o_ref[...] = _rms(y, gf_ref[...])


def _out_ffn(x2, u, attn_out, conv_w, conv_b, cln_g, cln_b, wc, wa, g2, w1, w2, gf, seq):
    n_tok = x2.shape[0]
    n_tiles = n_tok // TM_FFN
    const = lambda shape: pl.BlockSpec(shape, lambda i: (0, 0), pipeline_mode=pl.Buffered(1))
    return pl.pallas_call(
        functools.partial(_out_ffn_kernel, seq=seq),
        grid=(n_tiles,),
        in_specs=[
            pl.BlockSpec((TM_FFN, D_MODEL), lambda i: (i, 0)),
            const((TM_FFN, D_CONV)),
            pl.BlockSpec((TM_FFN, D_CONV), lambda i: (jnp.minimum(i + 1, n_tiles - 1), 0)),
            pl.BlockSpec((TM_FFN, D_ATTN), lambda i: (i, 0)),
            const((CONV_WIDTH * SUBLANES, D_CONV)),
            const((1, D_CONV)), const((1, D_CONV)), const((1, D_CONV)),
            const((D_CONV, D_MODEL)),
            const((D_ATTN, D_MODEL)),
            const((1, D_MODEL)),
            const((D_MODEL, D_FF)),
            const((D_FF, D_MODEL)),
            const((1, D_MODEL)),
        ],
        out_specs=pl.BlockSpec((TM_FFN, D_MODEL), lambda i: (i, 0)),
        out_shape=jax.ShapeDtypeStruct((n_tok, D_MODEL), F32),
        scratch_shapes=[
            pltpu.VMEM((TM_FFN, D_CONV), BF16),
            pltpu.VMEM((TM_FFN + CONV_HALO, D_CONV), F32),
            pltpu.VMEM((SUBLANES - 1, TM_FFN + CONV_HALO - SUBLANES, D_CONV), F32),
        ],
        compiler_params=pltpu.CompilerParams(
            dimension_semantics=("arbitrary",), vmem_limit_bytes=VMEM_LIMIT),
        name="out_ffn",
    )(x2, u, u, attn_out, conv_w, conv_b, cln_g, cln_b, wc, wa, g2, w1, w2, gf)


def _pack_w_in(w_in):
    o2 = 2 * D_CONV
    o3 = o2 + D_ATTN
    o4 = o3 + 2 * N_BRANCH * N_KV * HEAD_DIM
    gate_w = w_in[:, o4:].reshape(D_MODEL, N_HEADS, N_BRANCH).transpose(0, 2, 1)
    gate_w = gate_w.reshape(D_MODEL, N_BRANCH * N_HEADS)
    cols = [w_in[:, :o2], w_in[:, o2:o3] * HEAD_DIM ** -0.5, w_in[:, o3:o4], gate_w,
            jnp.zeros((D_MODEL, LANES - N_BRANCH * N_HEADS), w_in.dtype)]
    return jnp.concatenate(cols, axis=1).astype(BF16)


def _overlap_t(n_cmp_rows, n_sel):
    c = jnp.arange(n_cmp_rows)[None, :]
    j = jnp.arange(n_sel)[:, None]
    ov = (c * CMP_STRIDE < (j + 1) * SEL_BLOCK) & (c * CMP_STRIDE + CMP_BLOCK > j * SEL_BLOCK)
    return ov.astype(BF16)


def kernel(x, norm1_g, w_in, dw_w, dw_b, cln_g, cln_b, ck_pe, ck_w1, ck_w2, cv_pe, cv_w1, cv_w2,
           w_out, norm2_g, w_ff1, w_ff2, norm_f_g):
    batch, seq, _ = x.shape
    n_tok = batch * seq
    x2 = x.reshape(n_tok, D_MODEL)
    assert norm1_g.shape[0] == 1, "the final norm is fused into the (single) layer's last kernel"
    for l in range(1):
        w_a = _pack_w_in(w_in[l])
        u, q_pad, kvc_raw, k4, v4, gates = _proj_in(x2, norm1_g[l][None, :], w_a, seq)

        n_rows = seq // CMP_STRIDE
        a4 = kvc_raw.reshape(4, batch, seq, HEAD_DIM)
        pes = jnp.stack([ck_pe[l], cv_pe[l]]).reshape(2, 1, CMP_BLOCK * HEAD_DIM)
        w1s = jnp.stack([ck_w1[l], cv_w1[l]]).astype(BF16)
        zero2 = jnp.zeros_like(ck_w2[l])
        w2s = jnp.stack([jnp.concatenate([ck_w2[l], zero2], axis=1),
                         jnp.concatenate([cv_w2[l], cv_w2[l]], axis=1)]).astype(BF16)
        kvc = _compress(a4, pes, w1s, w2s)

        ovt = _overlap_t(n_rows, seq // SEL_BLOCK)
        attn_out = _attention(q_pad, kvc, k4, v4, gates, ovt, batch, seq)

        wo = w_out[l].astype(BF16)
        conv_w = jnp.repeat(dw_w[l][:, 0, :], SUBLANES, axis=0)
        x2 = _out_ffn(x2, u, attn_out, conv_w, dw_b[l][None, :], cln_g[l][None, :],
                      cln_b[l][None, :], wo[:D_CONV], wo[D_CONV:], norm2_g[l][None, :],
                      w_ff1[l].astype(BF16), w_ff2[l].astype(BF16), norm_f_g[None, :], seq)
    return x2.reshape(batch, seq, D_MODEL)
```

```python
import functools

import jax
import jax.numpy as jnp
from jax import lax
from jax.experimental import pallas as pl
from jax.experimental.pallas import tpu as pltpu

D_MODEL = 1024
D_CONV = 512
CONV_WIDTH = 31
N_HEADS = 8
HEAD_DIM = 64
N_KV = 2
HPG = N_HEADS // N_KV
D_ATTN = N_HEADS * HEAD_DIM
N_BRANCH = 3
CMP_BLOCK = 32
CMP_STRIDE = 16
CMP_HIDDEN = 256
SEL_BLOCK = 64
SEL_SHIFT = 6
SEL_TOP = 16
WINDOW = 512
D_FF = 4 * D_MODEL
EPS = 1e-6
NEG = -1e30
FORCE = 1e30
MASKED = -1e32

LANES = 128
SUBLANES = 8
F32 = jnp.float32
BF16 = jnp.bfloat16

TM_PROJ = 512
CONV_HALO = 32
CONV_ROWS = 16
TM_FFN = 512
SEL_TILE = 512
Q_TILE = 256
ROW_CHUNK = 64
RANK_WAYS = 2
F_WIN_BLK = HEAD_DIM
F_WIN_OFF = HEAD_DIM + 1
F_WIN_ONE = HEAD_DIM + 2
F_CMP_IDX = HEAD_DIM + 3
WIN_KEYS = WINDOW + Q_TILE
VMEM_LIMIT = 56 * 1024 * 1024

C_U = 0
C_Q = 2 * D_CONV
C_KVC = C_Q + D_ATTN
C_KV = C_KVC + 4 * HEAD_DIM
C_GATE = C_KV + 8 * HEAD_DIM
C_END = C_GATE + LANES

NT_DIMS = (((1,), (1,)), ((), ()))


def _dot(a, b):
    return jnp.dot(a, b, preferred_element_type=F32)


def _dot_nt(a, b):
    return lax.dot_general(a, b, NT_DIMS, preferred_element_type=F32)


def _proj_in_kernel(x_ref, g_ref, w_ref, u_ref, q_ref, kvc_ref, k_ref, v_ref, gate_ref, *, seq):
    i = pl.program_id(0)
    x = x_ref[...]
    ms = jnp.mean(x * x, axis=-1, keepdims=True)
    h = (x * lax.rsqrt(ms + EPS) * g_ref[...]).astype(BF16)

    zu = _dot(h, w_ref[:, C_U:C_Q])
    u_ref[...] = (zu[:, :D_CONV] * jax.nn.sigmoid(zu[:, D_CONV:])).astype(BF16)

    tm = x.shape[0]
    pad64 = jnp.zeros((tm, HEAD_DIM), F32)

    def emit_q():
        zq = _dot(h, w_ref[:, C_Q:C_KVC])
        for hd in range(N_HEADS):
            q_ref[:, hd * LANES:(hd + 1) * LANES] = jnp.concatenate(
                [zq[:, hd * HEAD_DIM:(hd + 1) * HEAD_DIM], pad64], axis=1).astype(BF16)

    def emit_kvc():
        zc = _dot(h, w_ref[:, C_KVC:C_KV])
        for n in range(4):
            kvc_ref[n] = zc[:, n * HEAD_DIM:(n + 1) * HEAD_DIM]

    def emit_kv():
        t = (i * tm) % seq + lax.broadcasted_iota(jnp.int32, (tm, LANES), 0)
        lane = lax.broadcasted_iota(jnp.int32, (tm, LANES), 1)
        feat = jnp.where(lane == LANES - 1, (t & (SEL_BLOCK - 1)).astype(F32),
                         jnp.where(lane - HEAD_DIM == (t >> SEL_SHIFT), 1.0, 0.0))
        win_feat = jnp.where(lane == F_WIN_BLK, (t >> SEL_SHIFT).astype(F32),
                             jnp.where(lane == F_WIN_OFF, (t & (SEL_BLOCK - 1)).astype(F32),
                                       jnp.where(lane == F_WIN_ONE, 1.0, 0.0)))
        ones64 = jnp.ones((tm, HEAD_DIM), F32)
        zkv = _dot(h, w_ref[:, C_KV:C_GATE])
        for n in range(8):
            z = zkv[:, n * HEAD_DIM:(n + 1) * HEAD_DIM]
            kind, grp = divmod(n, N_KV)
            if kind % 2 == 0:
                k = jnp.concatenate([z, pad64], axis=1) + (feat if kind == 0 else win_feat)
                k_ref[kind + grp] = k.astype(BF16)
            else:
                v_ref[kind - 1 + grp] = jnp.concatenate([z, ones64, ones64, z], axis=1).astype(BF16)

    emit_q()
    emit_kvc()
    emit_kv()
    gate_ref[...] = jax.nn.sigmoid(_dot(h, w_ref[:, C_GATE:C_END]))


def _proj_in(x2, g1, w_a, seq):
    n_tok = x2.shape[0]
    grid = (n_tok // TM_PROJ,)
    return pl.pallas_call(
        functools.partial(_proj_in_kernel, seq=seq),
        grid=grid,
        in_specs=[
            pl.BlockSpec((TM_PROJ, D_MODEL), lambda i: (i, 0)),
            pl.BlockSpec((1, D_MODEL), lambda i: (0, 0)),
            pl.BlockSpec((D_MODEL, C_END), lambda i: (0, 0), pipeline_mode=pl.Buffered(1)),
        ],
        out_specs=[
            pl.BlockSpec((TM_PROJ, D_CONV), lambda i: (i, 0)),
            pl.BlockSpec((TM_PROJ, N_HEADS * LANES), lambda i: (i, 0)),
            pl.BlockSpec((4, TM_PROJ, HEAD_DIM), lambda i: (0, i, 0)),
            pl.BlockSpec((4, TM_PROJ, LANES), lambda i: (0, i, 0)),
            pl.BlockSpec((4, TM_PROJ, 2 * LANES), lambda i: (0, i, 0)),
            pl.BlockSpec((TM_PROJ, LANES), lambda i: (i, 0)),
        ],
        out_shape=[
            jax.ShapeDtypeStruct((n_tok, D_CONV), BF16),
            jax.ShapeDtypeStruct((n_tok, N_HEADS * LANES), BF16),
            jax.ShapeDtypeStruct((4, n_tok, HEAD_DIM), F32),
            jax.ShapeDtypeStruct((4, n_tok, LANES), BF16),
            jax.ShapeDtypeStruct((4, n_tok, 2 * LANES), BF16),
            jax.ShapeDtypeStruct((n_tok, LANES), F32),
        ],
        compiler_params=pltpu.CompilerParams(
            dimension_semantics=("arbitrary",), vmem_limit_bytes=VMEM_LIMIT),
        name="proj_in",
    )(x2, g1, w_a)


def _compress_kernel(a_ref, pe_ref, w1_ref, w2_ref, o_ref):
    n_rows = a_ref.shape[2] // CMP_STRIDE
    a = jnp.concatenate(
        [a_ref[0, 0, pl.ds(i, n_rows, stride=CMP_STRIDE), :] for i in range(CMP_STRIDE)],
        axis=1).astype(BF16)
    half = CMP_STRIDE * HEAD_DIM
    top = _dot(a, w1_ref[0, :half, :])
    bot = _dot(a, w1_ref[0, half:, :])
    bot = pltpu.roll(bot, n_rows - 1, 0)
    pe = _dot(pe_ref[0].astype(BF16), w1_ref[0])
    hid = jax.nn.gelu(top + bot + pe, approximate=True)
    out = _dot(hid.astype(BF16), w2_ref[0])
    row = lax.broadcasted_iota(jnp.int32, out.shape, 0)
    lane = lax.broadcasted_iota(jnp.int32, out.shape, 1)
    is_key = pl.program_id(0) < N_KV
    out = out + jnp.where((lane == F_CMP_IDX) & is_key, row.astype(F32), 0.0)
    o_ref[0, 0] = jnp.where(row == n_rows - 1, 0.0, out).astype(BF16)


def _compress(a4, pes, w1s, w2s):
    _, batch, seq, _ = a4.shape
    n_rows = seq // CMP_STRIDE
    return pl.pallas_call(
        _compress_kernel,
        grid=(4, batch),
        in_specs=[
            pl.BlockSpec((1, 1, seq, HEAD_DIM), lambda n, b: (n, b, 0, 0)),
            pl.BlockSpec((1, 1, CMP_BLOCK * HEAD_DIM), lambda n, b: (n // N_KV, 0, 0)),
            pl.BlockSpec((1, CMP_BLOCK * HEAD_DIM, CMP_HIDDEN), lambda n, b: (n // N_KV, 0, 0)),
            pl.BlockSpec((1, CMP_HIDDEN, LANES), lambda n, b: (n // N_KV, 0, 0)),
        ],
        out_specs=pl.BlockSpec((1, 1, n_rows, LANES), lambda n, b: (n, b, 0, 0)),
        out_shape=jax.ShapeDtypeStruct((4, batch, n_rows, LANES), BF16),
        compiler_params=pltpu.CompilerParams(
            dimension_semantics=("arbitrary", "arbitrary"), vmem_limit_bytes=VMEM_LIMIT),
        name="compress",
    )(a4, pes, w1s, w2s)


def _conv_tile(u, first, w_ref, b_ref, g_ref, beta_ref, o_ref, ext_ref, sh_ref):
    tm = u.shape[0]
    ext_ref[0:CONV_HALO, :] = jnp.where(first, 0.0, ext_ref[tm:tm + CONV_HALO, :])
    ext_ref[CONV_HALO:, :] = u.astype(F32)
    off = CONV_HALO - (CONV_WIDTH - 1)
    n_sh = sh_ref.shape[1]
    for b in range(1, SUBLANES):
        sh_ref[b - 1] = ext_ref[b:b + n_sh, :]
    for r in range(tm // CONV_ROWS):
        r0 = r * CONV_ROWS
        acc = jnp.zeros((CONV_ROWS, D_CONV), F32) + b_ref[...]
        for k in range(CONV_WIDTH):
            a, b = divmod(off + k, SUBLANES)
            lo = r0 + a * SUBLANES
            tap = ext_ref[lo:lo + CONV_ROWS, :] if b == 0 else sh_ref[b - 1, lo:lo + CONV_ROWS, :]
            w_k = w_ref[k * SUBLANES:(k + 1) * SUBLANES, :]
            acc = acc + tap * jnp.concatenate([w_k] * (CONV_ROWS // SUBLANES), axis=0)
        mu = jnp.mean(acc, axis=-1, keepdims=True)
        d = acc - mu
        var = jnp.mean(d * d, axis=-1, keepdims=True)
        y = d * lax.rsqrt(var + EPS) * g_ref[...] + beta_ref[...]
        o_ref[r0:r0 + CONV_ROWS, :] = (y * jax.nn.sigmoid(y)).astype(BF16)


def _exp_rows(s):
    e = jnp.exp(s - jnp.max(s, axis=-1, keepdims=True))
    return e, 1.0 / jnp.sum(e, axis=-1, keepdims=True)


def _attn_kernel(q_ref, kc_ref, vc_ref, ksel_ref, vsel_ref, kwin_ref, vwin_ref,
                 gate_ref, ovt_ref, gexp_ref, o_ref,
                 qa_ref, m_ref, acc_ref, s0_ref, s1_ref, p0_ref, p1_ref,
                 sw_ref, ew_ref, sc_ref, ec_ref, tiles_ref):
    g = pl.program_id(1)
    step = pl.program_id(2)
    q0 = step * Q_TILE
    ci0 = step * (Q_TILE // SEL_BLOCK)
    rows = HPG * Q_TILE
    qb = q_ref[...]
    q_heads = [qb[:, h * LANES:(h + 1) * LANES] for h in range(HPG)]
    slope0 = jnp.where(g == 0, 0.5, 0.5 ** (HPG + 1)).astype(F32)
    slopes = [slope0 * (0.5 ** h) for h in range(HPG)]
    head_rows = lambda a, h: a[h * Q_TILE:(h + 1) * Q_TILE]

    lane = lax.broadcasted_iota(jnp.int32, (Q_TILE, LANES), 1)
    n_rc = Q_TILE // ROW_CHUNK


    win_blk0 = jnp.maximum(ci0 - WINDOW // SEL_BLOCK, 0)
    start = pl.multiple_of(win_blk0 * SEL_BLOCK, SEL_BLOCK)
    q_cw = []
    for h in range(HPG):
        feat = jnp.where(
            lane == F_WIN_BLK, SEL_BLOCK * slopes[h],
            jnp.where(lane == F_WIN_OFF, slopes[h],
                      jnp.where(lane == F_WIN_ONE, -SEL_BLOCK * slopes[h] * win_blk0.astype(F32),
                                jnp.where(lane == F_CMP_IDX, CMP_STRIDE * slopes[h], 0.0))))
        q_cw.append(jnp.where(lane < HEAD_DIM, q_heads[h].astype(F32), feat).astype(BF16))
    q_cw = jnp.concatenate(q_cw, axis=0)

    n_cmp = kc_ref.shape[2]
    sc_ref[...] = _dot_nt(q_cw, kc_ref[0, 0])
    r_cmp = [[None] * n_rc for _ in range(HPG)]
    p_sum = []
    for c in range(n_rc):
        rq = c * ROW_CHUNK + lax.broadcasted_iota(jnp.int32, (ROW_CHUNK, n_cmp), 0)
        cc = lax.broadcasted_iota(jnp.int32, (ROW_CHUNK, n_cmp), 1)
        valid = q0 + rq - cc * CMP_STRIDE >= CMP_BLOCK - 1
        p_chunk = jnp.zeros((ROW_CHUNK, n_cmp), F32)
        for h in range(HPG):
            r0 = h * Q_TILE + c * ROW_CHUNK
            e, r_sum = _exp_rows(jnp.where(valid, sc_ref[r0:r0 + ROW_CHUNK, :], NEG))
            p_chunk = p_chunk + e * r_sum
            ec_ref[r0:r0 + ROW_CHUNK, :] = e.astype(BF16)
            r_cmp[h][c] = r_sum
        p_sum.append(p_chunk)
    p_sum = jnp.concatenate(p_sum, axis=0)
    o_cmp_all = _dot(ec_ref[...], vc_ref[0, 0])

    p_hi = p_sum.astype(BF16)
    p_lo = (p_sum - p_hi.astype(F32)).astype(BF16)
    ovt = ovt_ref[...]
    imp = _dot_nt(ovt, p_hi) + _dot_nt(ovt, p_lo)
    n_sel = imp.shape[0]
    jrow = lax.broadcasted_iota(jnp.int32, (n_sel, Q_TILE), 0)
    qlane = lax.broadcasted_iota(jnp.int32, (n_sel, Q_TILE), 1)
    cl = ci0 + (qlane >> SEL_SHIFT)
    forced = (jrow == 0) | (jrow == cl) | (jrow == cl - 1)
    imp = jnp.where(forced, FORCE, jnp.where(jrow > cl, NEG, imp))
    sw_ref[...] = _dot_nt(q_cw, kwin_ref[0, pl.ds(start, WIN_KEYS), :])
    groups = [imp[8 * v:8 * v + 8, :] for v in range(n_sel // 8)]
    sub = lax.broadcasted_iota(jnp.int32, (8, Q_TILE), 0)
    ranks = [[jnp.zeros((8, Q_TILE), jnp.int32) for _ in range(RANK_WAYS)] for _ in groups]
    for jp in range(n_sel):
        row = groups[jp // 8][jp % 8:jp % 8 + 1, :]
        w = jp % RANK_WAYS
        for v in range(n_sel // 8):
            x = groups[v]
            if 8 * v + 7 < jp:
                ranks[v][w] = ranks[v][w] + jnp.where(row > x, 1, 0)
            elif 8 * v > jp:
                ranks[v][w] = ranks[v][w] + jnp.where(row >= x, 1, 0)
            else:
                tie = jnp.where(sub + 8 * v > jp, 1, 0)
                ranks[v][w] = (ranks[v][w] + jnp.where(row > x, 1, 0)
                               + jnp.where(row == x, tie, 0))
    rank = jnp.concatenate([sum(r[1:], r[0]) for r in ranks], axis=0)
    sel_t = jnp.where((rank < SEL_TOP) & (jrow <= cl), 1.0, 0.0)

    tile_blocks = SEL_TILE // SEL_BLOCK
    n_full = (ci0 + Q_TILE // SEL_BLOCK - 1) // tile_blocks
    n_active = jnp.int32(0)
    for t in range(n_sel // tile_blocks):
        wanted = jnp.max(sel_t[t * tile_blocks:(t + 1) * tile_blocks, :]) > 0.5
        tiles_ref[n_active] = t
        n_active = n_active + jnp.where(wanted & (t < n_full), 1, 0)
    tiles_ref[n_active] = n_full

    sel_t = jnp.concatenate([jnp.zeros((LANES - n_sel, Q_TILE), F32), sel_t], axis=0)
    sel_q = jnp.concatenate(
        [sel_t[:, c * LANES:(c + 1) * LANES].T for c in range(Q_TILE // LANES)], axis=0)
    qrow = lax.broadcasted_iota(jnp.int32, (Q_TILE, LANES), 0)
    base = ((lane - HEAD_DIM - ci0 - (qrow >> SEL_SHIFT)) * SEL_BLOCK).astype(F32)
    q_aug = []
    for h in range(HPG):
        feat = jnp.where(lane == LANES - 1, slopes[h],
                         jnp.where(sel_q > 0.5, slopes[h] * base, MASKED))
        q_aug.append(jnp.where(lane < HEAD_DIM, q_heads[h].astype(F32), feat).astype(BF16))
    q_aug = jnp.concatenate(q_aug, axis=0)

    def k_tile(kt):
        return ksel_ref[0, pl.ds(pl.multiple_of(kt * SEL_TILE, SEL_TILE), SEL_TILE), :]

    def v_tile(kt):
        return vsel_ref[0, pl.ds(pl.multiple_of(kt * SEL_TILE, SEL_TILE), SEL_TILE), :]

    s_bufs = (s0_ref, s1_ref)
    p_bufs = (p0_ref, p1_ref)
    qa_ref[...] = q_aug
    m_ref[...] = jnp.full((rows, LANES), NEG, F32)
    acc_ref[...] = jnp.zeros((rows, 2 * LANES), F32)
    p1_ref[...] = jnp.zeros((rows, SEL_TILE), BF16)
    s0_ref[...] = _dot_nt(q_aug, k_tile(tiles_ref[0]))

    o_win_all = []
    for pair in range(HPG // 2):
        for c in range(n_rc):
            rq = c * ROW_CHUNK + lax.broadcasted_iota(jnp.int32, (ROW_CHUNK, WIN_KEYS), 0)
            dw = q0 + rq - start - lax.broadcasted_iota(jnp.int32, (ROW_CHUNK, WIN_KEYS), 1)
            valid_w = (dw | (WINDOW - 1 - dw)) >= 0
            for h in (2 * pair, 2 * pair + 1):
                r0 = h * Q_TILE + c * ROW_CHUNK
                s = jnp.where(valid_w, sw_ref[r0:r0 + ROW_CHUNK, :], NEG)
                e = jnp.exp(s - jnp.max(s, axis=-1, keepdims=True))
                ew_ref[r0:r0 + ROW_CHUNK, :] = e.astype(BF16)
        pair_rows = slice(2 * pair * Q_TILE, (2 * pair + 2) * Q_TILE)
        o_win_all.append(_dot(ew_ref[pair_rows, :], vwin_ref[0, pl.ds(start, WIN_KEYS), :]))
    o_win_all = jnp.concatenate(o_win_all, axis=0)

    def online(s_ref, p_ref, pv, causal_tile, width=SEL_TILE):
        for c in range(n_rc):
            if causal_tile:
                kpos = n_full * SEL_TILE + lax.broadcasted_iota(jnp.int32, (ROW_CHUNK, width), 1)
                tq = q0 + c * ROW_CHUNK + lax.broadcasted_iota(jnp.int32, (ROW_CHUNK, width), 0)
                causal = kpos <= tq
            for h in range(HPG):
                r0 = h * Q_TILE + c * ROW_CHUNK
                rs = slice(r0, r0 + ROW_CHUNK)
                s = s_ref[rs, :width]
                if causal_tile:
                    s = jnp.where(causal, s, MASKED)
                m_old = m_ref[rs, :]
                m_new = jnp.maximum(m_old, jnp.max(s, axis=-1, keepdims=True))
                m_ref[rs, :] = m_new
                p = [jnp.exp(s[:, j * LANES:(j + 1) * LANES] - m_new)
                     for j in range(width // LANES)]
                p_ref[rs, :width] = jnp.concatenate(p, axis=1).astype(BF16)
                alpha = jnp.exp(m_old - m_new)
                acc_ref[rs, :] = jnp.concatenate([alpha, alpha], axis=1) * (acc_ref[rs, :]
                                                                           + pv[rs, :])

    def stage(i, par):
        s_bufs[1 - par][...] = _dot_nt(qa_ref[...], k_tile(tiles_ref[i + 1]))
        pv = _dot(p_bufs[1 - par][...], v_tile(tiles_ref[jnp.maximum(i - 1, 0)]))
        online(s_bufs[par], p_bufs[par], pv, False)

    def sweep(i, carry):
        lax.cond(i % 2 == 0, lambda: stage(i, 0), lambda: stage(i, 1))
        return carry

    lax.fori_loop(0, n_active, sweep, 0)

    def last(par, width):
        pv = _dot(p_bufs[1 - par][...], v_tile(tiles_ref[jnp.maximum(n_active - 1, 0)]))
        online(s_bufs[par], p_bufs[par], pv, True, width)
        own_v = vsel_ref[0, pl.ds(pl.multiple_of(n_full * SEL_TILE, SEL_TILE), width), :]
        acc_ref[...] += _dot(p_bufs[par][:, :width], own_v)

    def finish(width):
        lax.cond(n_active % 2 == 0, lambda: last(0, width), lambda: last(1, width))

    own_last = (ci0 + Q_TILE // SEL_BLOCK - 1) % tile_blocks
    lax.cond(own_last < tile_blocks // 2, lambda: finish(SEL_TILE // 2), lambda: finish(SEL_TILE))
    o_sel_all = acc_ref[...]

    gates = gate_ref[...]
    g_hi = gates.astype(BF16)
    g_lo = (gates - g_hi.astype(F32)).astype(BF16)
    expand = gexp_ref[...]
    g_wide = _dot(g_hi, expand) + _dot(g_lo, expand)
    even_lanes = lane < HEAD_DIM
    sees_cmp = q0 + lax.broadcasted_iota(jnp.int32, (Q_TILE, LANES), 0) >= CMP_BLOCK - 1
    pairs = []
    for pair in range(HPG // 2):
        h0, h1 = 2 * pair, 2 * pair + 1
        def gate(n, pair=pair):
            blk = n * (HPG // 2) + pair
            return g_wide[:, blk * LANES:(blk + 1) * LANES]
        def both(x0, x1):
            return jnp.where(even_lanes, x0, x1)
        o_cmp = both(head_rows(o_cmp_all, h0) * jnp.concatenate(r_cmp[h0], axis=0),
                     head_rows(o_cmp_all, h1) * jnp.concatenate(r_cmp[h1], axis=0))
        o_cmp = jnp.where(sees_cmp, o_cmp, 0.0)
        def normalised(acc_all):
            a0, a1 = head_rows(acc_all, h0), head_rows(acc_all, h1)
            return both(a0[:, :LANES], a1[:, LANES:]) * (1.0 / both(a0[:, LANES:], a1[:, :LANES]))
        pairs.append(gate(0) * o_cmp + gate(1) * normalised(o_sel_all)
                     + gate(2) * normalised(o_win_all))
    o_ref[...] = jnp.concatenate(pairs, axis=1).astype(BF16)


def _gate_expand():
    col = jnp.arange(N_KV * N_BRANCH * (HPG // 2) * LANES)
    blk, h_in_pair = col // LANES, (col % LANES) // HEAD_DIM
    g, n, pair = blk // (N_BRANCH * HPG // 2), (blk // (HPG // 2)) % N_BRANCH, blk % (HPG // 2)
    row = jnp.arange(LANES)[:, None]
    return (row == (n * N_HEADS + g * HPG + 2 * pair + h_in_pair)[None, :]).astype(BF16)


def _attention(q_pad, kvc, k4, v4, gates, ovt, batch, seq):
    n_steps = seq // Q_TILE
    rows = HPG * Q_TILE
    n_cmp = kvc.shape[2]
    n_sel = ovt.shape[0]
    k_spec = lambda branch: pl.BlockSpec(
        (1, seq, LANES), lambda b, g, c, branch=branch: (branch * N_KV + g, b, 0))
    v_spec = lambda branch: pl.BlockSpec(
        (1, seq, 2 * LANES), lambda b, g, c, branch=branch: (branch * N_KV + g, b, 0))
    return pl.pallas_call(
        _attn_kernel,
        grid=(batch, N_KV, n_steps),
        in_specs=[
            pl.BlockSpec((Q_TILE, HPG * LANES), lambda b, g, c: (b * n_steps + c, g)),
            pl.BlockSpec((1, 1, n_cmp, LANES), lambda b, g, c: (g, b, 0, 0)),
            pl.BlockSpec((1, 1, n_cmp, LANES), lambda b, g, c: (N_KV + g, b, 0, 0)),
            k_spec(0), v_spec(0), k_spec(1), v_spec(1),
            pl.BlockSpec((Q_TILE, LANES), lambda b, g, c: (b * n_steps + c, 0)),
            pl.BlockSpec((n_sel, n_cmp), lambda b, g, c: (0, 0)),
            pl.BlockSpec((LANES, N_BRANCH * (HPG // 2) * LANES), lambda b, g, c: (0, g)),
        ],
        out_specs=pl.BlockSpec((Q_TILE, HPG * HEAD_DIM), lambda b, g, c: (b * n_steps + c, g)),
        out_shape=jax.ShapeDtypeStruct((batch * seq, D_ATTN), BF16),
        scratch_shapes=[
            pltpu.VMEM((rows, LANES), BF16),
            pltpu.VMEM((rows, LANES), F32),
            pltpu.VMEM((rows, 2 * LANES), F32),
            pltpu.VMEM((rows, SEL_TILE), F32), pltpu.VMEM((rows, SEL_TILE), F32),
            pltpu.VMEM((rows, SEL_TILE), BF16), pltpu.VMEM((rows, SEL_TILE), BF16),
            pltpu.VMEM((rows, WIN_KEYS), F32), pltpu.VMEM((rows, WIN_KEYS), BF16),
            pltpu.VMEM((rows, n_cmp), F32), pltpu.VMEM((rows, n_cmp), BF16),
            pltpu.SMEM((seq // SEL_TILE + 1,), jnp.int32),
        ],
        compiler_params=pltpu.CompilerParams(
            dimension_semantics=("arbitrary", "arbitrary", "arbitrary"),
            vmem_limit_bytes=VMEM_LIMIT),
        name="attn",
    )(q_pad, kvc, kvc, k4, v4, k4, v4, gates, ovt, _gate_expand())


def _rms(x, g):
    return x * lax.rsqrt(jnp.mean(x * x, axis=-1, keepdims=True) + EPS) * g


def _out_ffn_kernel(x_ref, u0_ref, un_ref, a_ref, cw_ref, cb_ref, cg_ref, cbeta_ref,
                    wc_ref, wa_ref, g2_ref, w1_ref, w2_ref, gf_ref, o_ref,
                    conv_ref, ext_ref, sh_ref, *, seq):
    i = pl.program_id(0)
    tm = x_ref.shape[0]
    conv_args = (cw_ref, cb_ref, cg_ref, cbeta_ref, conv_ref, ext_ref, sh_ref)

    @pl.when(i == 0)
    def _():
        ext_ref[...] = jnp.zeros(ext_ref.shape, F32)
        _conv_tile(u0_ref[...], True, *conv_args)

    conv_cur = conv_ref[...]
    _conv_tile(un_ref[...], ((i + 1) * tm) % seq == 0, *conv_args)
    x1 = x_ref[...] + _dot(conv_cur, wc_ref[...]) + _dot(a_ref[...], wa_ref[...])
    h2 = _rms(x1, g2_ref[...]).astype(BF16)
    a = jnp.maximum(_dot(h2, w1_ref[...]), 0.0)
    y = x1 + _dot((a * a).astype(BF16), w2_ref[...])
    o_ref[...] = _rms(y, gf_ref[...])


def _out_ffn(x2, u, attn_out, conv_w, conv_b, cln_g, cln_b, wc, wa, g2, w1, w2, gf, seq):
    n_tok = x2.shape[0]
    n_tiles = n_tok // TM_FFN
    const = lambda shape: pl.BlockSpec(shape, lambda i: (0, 0), pipeline_mode=pl.Buffered(1))
    return pl.pallas_call(
        functools.partial(_out_ffn_kernel, seq=seq),
        grid=(n_tiles,),
        in_specs=[
            pl.BlockSpec((TM_FFN, D_MODEL), lambda i: (i, 0)),
            const((TM_FFN, D_CONV)),
            pl.BlockSpec((TM_FFN, D_CONV), lambda i: (jnp.minimum(i + 1, n_tiles - 1), 0)),
            pl.BlockSpec((TM_FFN, D_ATTN), lambda i: (i, 0)),
            const((CONV_WIDTH * SUBLANES, D_CONV)),
            const((1, D_CONV)), const((1, D_CONV)), const((1, D_CONV)),
            const((D_CONV, D_MODEL)),
            const((D_ATTN, D_MODEL)),
            const((1, D_MODEL)),
            const((D_MODEL, D_FF)),
            const((D_FF, D_MODEL)),
            const((1, D_MODEL)),
        ],
        out_specs=pl.BlockSpec((TM_FFN, D_MODEL), lambda i: (i, 0)),
        out_shape=jax.ShapeDtypeStruct((n_tok, D_MODEL), F32),
        scratch_shapes=[
            pltpu.VMEM((TM_FFN, D_CONV), BF16),
            pltpu.VMEM((TM_FFN + CONV_HALO, D_CONV), F32),
            pltpu.VMEM((SUBLANES - 1, TM_FFN + CONV_HALO - SUBLANES, D_CONV), F32),
        ],
        compiler_params=pltpu.CompilerParams(
            dimension_semantics=("arbitrary",), vmem_limit_bytes=VMEM_LIMIT),
        name="out_ffn",
    )(x2, u, u, attn_out, conv_w, conv_b, cln_g, cln_b, wc, wa, g2, w1, w2, gf)


def _pack_w_in(w_in):
    o2 = 2 * D_CONV
    o3 = o2 + D_ATTN
    o4 = o3 + 2 * N_BRANCH * N_KV * HEAD_DIM
    gate_w = w_in[:, o4:].reshape(D_MODEL, N_HEADS, N_BRANCH).transpose(0, 2, 1)
    gate_w = gate_w.reshape(D_MODEL, N_BRANCH * N_HEADS)
    cols = [w_in[:, :o2], w_in[:, o2:o3] * HEAD_DIM ** -0.5, w_in[:, o3:o4], gate_w,
            jnp.zeros((D_MODEL, LANES - N_BRANCH * N_HEADS), w_in.dtype)]
    return jnp.concatenate(cols, axis=1).astype(BF16)


def _overlap_t(n_cmp_rows, n_sel):
    c = jnp.arange(n_cmp_rows)[None, :]
    j = jnp.arange(n_sel)[:, None]
    ov = (c * CMP_STRIDE < (j + 1) * SEL_BLOCK) & (c * CMP_STRIDE + CMP_BLOCK > j * SEL_BLOCK)
    return ov.astype(BF16)


def kernel(x, norm1_g, w_in, dw_w, dw_b, cln_g, cln_b, ck_pe, ck_w1, ck_w2, cv_pe, cv_w1, cv_w2,
           w_out, norm2_g, w_ff1, w_ff2, norm_f_g):
    batch, seq, _ = x.shape
    n_tok = batch * seq
    x2 = x.reshape(n_tok, D_MODEL)
    assert norm1_g.shape[0] == 1, "the final norm is fused into the (single) layer's last kernel"
    for l in range(1):
        w_a = _pack_w_in(w_in[l])
        u, q_pad, kvc_raw, k4, v4, gates = _proj_in(x2, norm1_g[l][None, :], w_a, seq)

        n_rows = seq // CMP_STRIDE
        a4 = kvc_raw.reshape(4, batch, seq, HEAD_DIM)
        pes = jnp.stack([ck_pe[l], cv_pe[l]]).reshape(2, 1, CMP_BLOCK * HEAD_DIM)
        w1s = jnp.stack([ck_w1[l], cv_w1[l]]).astype(BF16)
        zero2 = jnp.zeros_like(ck_w2[l])
        w2s = jnp.stack([jnp.concatenate([ck_w2[l], zero2], axis=1),
                         jnp.concatenate([cv_w2[l], cv_w2[l]], axis=1)]).astype(BF16)
        kvc = _compress(a4, pes, w1s, w2s)

        ovt = _overlap_t(n_rows, seq // SEL_BLOCK)
        attn_out = _attention(q_pad, kvc, k4, v4, gates, ovt, batch, seq)

        wo = w_out[l].astype(BF16)
        conv_w = jnp.repeat(dw_w[l][:, 0, :], SUBLANES, axis=0)
        x2 = _out_ffn(x2, u, attn_out, conv_w, dw_b[l][None, :], cln_g[l][None, :],
                      cln_b[l][None, :], wo[:D_CONV], wo[D_CONV:], norm2_g[l][None, :],
                      w_ff1[l].astype(BF16), w_ff2[l].astype(BF16), norm_f_g[None, :], seq)
    return x2.reshape(batch, seq, D_MODEL)
```
